```python
import jax, jax.numpy as jnp
from jax import lax
import numpy as np

D_MODEL = 2048
BATCH = 2
SEQ = 4096
DEPTH = 4
DEC_BATCH = 8
DEC_SEQ = 1
PAST_LEN = 16384
PAGE_SIZE = 128

FOX_HEADS = 8
FOX_DH = D_MODEL // 16
FOX_W = FOX_HEADS * FOX_DH
GLA_HEADS = 4
GLA_DK = D_MODEL // 16
GLA_DV = D_MODEL // 8
GLA_KW = GLA_HEADS * GLA_DK
GLA_VW = GLA_HEADS * GLA_DV
GLA_RANK = 16
GLA_TAU = 16.0
RET_HEADS = 4
RET_DH = D_MODEL // 8
RET_W = RET_HEADS * RET_DH
MEM_HEADS = 4
MEM_DH = D_MODEL // 8
MEM_W = MEM_HEADS * MEM_DH
N_MEM = 256
N_BRANCH = 4
BRANCH_W = D_MODEL // 2
D_FF = 4 * D_MODEL
D_IN = 3 * FOX_W + FOX_HEADS + 2 * GLA_KW + 2 * GLA_VW + GLA_RANK + 4 * RET_W + MEM_W + N_BRANCH * D_MODEL
Q_BLOCK = 128
CHUNK = 64
ROPE_BASE = 10000.0
EPS = 1e-6
FOX_BIAS_INIT = 2.0

kernel_name = 'hybrid_fox_gla_retnet_decoder_step'


def rmsnorm(x, g):
    xf = x.astype(jnp.float32)
    y = xf * lax.rsqrt(jnp.mean(xf * xf, axis=-1, keepdims=True) + EPS)
    return (y * g.astype(jnp.float32)).astype(x.dtype)


def head_norm(o, g):
    mu = jnp.mean(o, axis=-1, keepdims=True)
    var = jnp.mean(jnp.square(o - mu), axis=-1, keepdims=True)
    y = (o - mu) * lax.rsqrt(var + EPS)
    return y.reshape(*o.shape[:-2], -1) * g.astype(jnp.float32)


def rotary(x, pos):
    half = x.shape[-1] // 2
    inv = ROPE_BASE ** (-jnp.arange(half, dtype=jnp.float32) / half)
    ang = pos.astype(jnp.float32)[:, None] * inv[None, :]
    cos = jnp.cos(ang)[None, :, None, :]
    sin = jnp.sin(ang)[None, :, None, :]
    xf = x.astype(jnp.float32)
    x1, x2 = xf[..., :half], xf[..., half:]
    return jnp.concatenate([x1 * cos - x2 * sin, x1 * sin + x2 * cos], axis=-1).astype(x.dtype)


def retention_log_decay():
    return jnp.log1p(-jnp.exp2(-5.0 - jnp.arange(RET_HEADS, dtype=jnp.float32)))


def in_proj(h, pos, w_in, b_fox_f, w_gla_a2, b_gla_a):
    B, T, _ = h.shape
    z = h @ w_in
    sizes = (FOX_W, FOX_W, FOX_W, FOX_HEADS, GLA_KW, GLA_KW, GLA_VW, GLA_VW, GLA_RANK,
             RET_W, RET_W, RET_W, RET_W, MEM_W, N_BRANCH * D_MODEL)
    cuts = [int(c) for c in np.cumsum(sizes)[:-1]]
    fq, fk, fv, ff, gq, gk, gv, gr, ga, rq, rk, rv, rg, mq, gt = jnp.split(z, cuts, axis=-1)
    heads = lambda a, n: a.reshape(B, T, n, -1)
    fox_logf = jax.nn.log_sigmoid((ff + b_fox_f).astype(jnp.float32))
    fox = (heads(fq, FOX_HEADS) * FOX_DH ** -0.5, heads(fk, FOX_HEADS), heads(fv, FOX_HEADS), fox_logf)
    gla_loga = jax.nn.log_sigmoid((ga @ w_gla_a2 + b_gla_a).astype(jnp.float32)) / GLA_TAU
    gla = (heads(gq, GLA_HEADS) * GLA_DK ** -0.5, heads(gk, GLA_HEADS), heads(gv, GLA_HEADS),
           gr, heads(gla_loga, GLA_HEADS))
    ret = (rotary(heads(rq, RET_HEADS), pos), rotary(heads(rk, RET_HEADS), pos) * RET_DH ** -0.5,
           heads(rv, RET_HEADS), rg)
    mem_q = heads(mq, MEM_HEADS) * MEM_DH ** -0.5
    gates = jax.nn.sigmoid(gt).reshape(B, T, N_BRANCH, D_MODEL)
    return fox, gla, ret, mem_q, gates


def fox_attention(q, k, v, logf, n_past):
    B, Tq, H, Dh = q.shape
    Tk = k.shape[1]
    c = lax.cumsum(logf.astype(jnp.float32), axis=1)
    c_keys = jnp.swapaxes(c, 1, 2)
    c_q = c[:, n_past:]
    kpos = jnp.arange(Tk)
    qpos = n_past + jnp.arange(Tq)

    def attend(args):
        qb, cb, pb = args
        s = jnp.einsum('bqhd,bkhd->bhqk', qb, k).astype(jnp.float32)
        s = s + (jnp.swapaxes(cb, 1, 2)[..., :, None] - c_keys[..., None, :])
        s = jnp.where(kpos[None, :] <= pb[:, None], s, -jnp.inf)
        p = jax.nn.softmax(s, axis=-1).astype(v.dtype)
        return jnp.einsum('bhqk,bkhd->bqhd', p, v)

    if Tq % Q_BLOCK == 0:
        nb = Tq // Q_BLOCK
        qb = jnp.swapaxes(q.reshape(B, nb, Q_BLOCK, H, Dh), 0, 1)
        cb = jnp.swapaxes(c_q.reshape(B, nb, Q_BLOCK, H), 0, 1)
        pb = qpos.reshape(nb, Q_BLOCK)
        o = jnp.swapaxes(lax.map(attend, (qb, cb, pb)), 0, 1)
    else:
        o = attend((q, c_q, qpos))
    return o.reshape(B, Tq, H * Dh)


def gla_chunk(q, k, v, loga, S0):
    q, k, v = (a.astype(jnp.float32) for a in (q, k, v))
    L = q.shape[1]
    b = lax.cumsum(loga, axis=1)
    causal = jnp.tril(jnp.ones((L, L), dtype=bool))
    rel = jnp.where(causal[None, :, :, None, None], b[:, :, None] - b[:, None, :], -jnp.inf)
    A = jnp.sum(q[:, :, None] * k[:, None, :] * jnp.exp(rel), axis=-1)
    o = jnp.einsum('bqkh,bkhv->bqhv', A, v) + jnp.einsum('bqhd,bhdv->bqhv', q * jnp.exp(b), S0)
    k_end = k * jnp.exp(b[:, -1:] - b)
    S = jnp.exp(b[:, -1])[..., None] * S0 + jnp.einsum('bkhd,bkhv->bhdv', k_end, v)
    return o, S


def ret_chunk(q, k, v, S0):
    q, k, v = (a.astype(jnp.float32) for a in (q, k, v))
    L = q.shape[1]
    lg = retention_log_decay()
    t = jnp.arange(L, dtype=jnp.float32)
    causal = jnp.tril(jnp.ones((L, L), dtype=bool))
    rel = jnp.where(causal[..., None], (t[:, None] - t[None, :])[..., None] * lg, -jnp.inf)
    A = jnp.einsum('bqhd,bkhd->bqkh', q, k) * jnp.exp(rel)[None]
    o = (jnp.einsum('bqkh,bkhv->bqhv', A, v)
         + jnp.exp((t + 1.0)[:, None] * lg)[None, :, :, None] * jnp.einsum('bqhd,bhdv->bqhv', q, S0))
    k_end = k * jnp.exp((L - 1.0 - t)[:, None] * lg)[None, :, :, None]
    S = jnp.exp(L * lg)[None, :, None, None] * S0 + jnp.einsum('bkhd,bkhv->bhdv', k_end, v)
    return o, S


def scan_chunks(chunk_fn, xs, S0):
    B, T = xs[0].shape[:2]
    L = CHUNK if T % CHUNK == 0 else T
    n = T // L
    split = lambda a: jnp.swapaxes(a.reshape(B, n, L, *a.shape[2:]), 0, 1)

    def step(S, blk):
        o, S = chunk_fn(*blk, S)
        return S, o

    S, o = lax.scan(step, S0, tuple(split(a) for a in xs))
    o = jnp.swapaxes(o, 0, 1).reshape(B, T, *o.shape[3:])
    return o, S


def mem_attend(q, mk, mv):
    s = jnp.einsum('bqhd,bkhd->bhqk', q, mk).astype(jnp.float32)
    p = jax.nn.softmax(s, axis=-1).astype(mv.dtype)
    o = jnp.einsum('bhqk,bkhd->bqhd', p, mv)
    return o.reshape(*o.shape[:2], -1)


def layer(x, pos, fox_past, gla_S0, ret_S0, mem_k, mem_v, g_mix, w_in, b_fox_f, w_gla_a2, b_gla_a,
          g_gla, g_ret, w_branch, w_out, g_mlp, w_ff1, w_ff2):
    h = rmsnorm(x, g_mix)
    (fq, fk, fv, flogf), (gq, gk, gv, gr, ga), (rq, rk, rv, rg), mq, gates = in_proj(
        h, pos, w_in, b_fox_f, w_gla_a2, b_gla_a)
    if fox_past is None:
        k_all, v_all, lf_all, n_past = fk, fv, flogf, 0
    else:
        kp, vp, lp = fox_past
        k_all = jnp.concatenate([kp.astype(fk.dtype), fk], axis=1)
        v_all = jnp.concatenate([vp.astype(fv.dtype), fv], axis=1)
        lf_all = jnp.concatenate([lp.astype(jnp.float32), flogf], axis=1)
        n_past = kp.shape[1]
    o_fox = fox_attention(fq, k_all, v_all, lf_all, n_past).astype(x.dtype)
    o_gla, S_gla = scan_chunks(gla_chunk, (gq, gk, gv, ga), gla_S0.astype(jnp.float32))
    o_gla = (head_norm(o_gla, g_gla) * jax.nn.silu(gr.astype(jnp.float32))).astype(x.dtype)
    o_ret, S_ret = scan_chunks(ret_chunk, (rq, rk, rv), ret_S0.astype(jnp.float32))
    o_ret = (head_norm(o_ret, g_ret) * jax.nn.silu(rg.astype(jnp.float32))).astype(x.dtype)
    o_mem = mem_attend(mq, mem_k.astype(mq.dtype), mem_v.astype(mq.dtype)).astype(x.dtype)
    o = jnp.stack([o_fox, o_gla, o_ret, o_mem], axis=2)
    merged = jnp.sum(gates * jnp.einsum('btnc,ncd->btnd', o, w_branch), axis=2)
    x = x + merged @ w_out
    h2 = rmsnorm(x, g_mlp)
    x = x + jnp.square(jax.nn.relu(h2 @ w_ff1)) @ w_ff2
    return x, (fk, fv, flogf), S_gla, S_ret


def gather_pages(pool_l, page_table):
    g = pool_l[page_table]
    return g.reshape(page_table.shape[0], page_table.shape[1] * PAGE_SIZE, *pool_l.shape[2:])


def setup_inputs(seed: int = 0) -> dict:
    key = jax.random.key(seed)
    keys = jax.random.split(key, 26)

    def nrm(i, shape, scale=1.0):
        return jax.random.normal(keys[i], shape, jnp.float32) * scale

    n_pages = PAST_LEN // PAGE_SIZE
    n_used = DEC_BATCH * n_pages
    n_pool = n_used + max(n_used // 4, 1)
    perm = jax.random.permutation(keys[0], n_pool)
    page_table = perm[:n_used].reshape(DEC_BATCH, n_pages).astype(jnp.int32)
    return {
        'x_prompt': nrm(1, (BATCH, SEQ, D_MODEL)),
        'x_sample': nrm(2, (DEC_BATCH, DEC_SEQ, D_MODEL)),
        'cache_fox_k': nrm(3, (DEPTH, n_pool, PAGE_SIZE, FOX_HEADS, FOX_DH)),
        'cache_fox_v': nrm(4, (DEPTH, n_pool, PAGE_SIZE, FOX_HEADS, FOX_DH)),
        'cache_fox_logf': jax.nn.log_sigmoid(FOX_BIAS_INIT + nrm(5, (DEPTH, n_pool, PAGE_SIZE, FOX_HEADS))),
        'state_gla': nrm(6, (DEPTH, DEC_BATCH, GLA_HEADS, GLA_DK, GLA_DV)),
        'state_ret': nrm(7, (DEPTH, DEC_BATCH, RET_HEADS, RET_DH, RET_DH)),
        'cache_mem_k': nrm(8, (DEPTH, DEC_BATCH, N_MEM, MEM_HEADS, MEM_DH)),
        'cache_mem_v': nrm(9, (DEPTH, DEC_BATCH, N_MEM, MEM_HEADS, MEM_DH)),
        'page_table': page_table,
        'mem_prompt': nrm(10, (BATCH, N_MEM, D_MODEL)),
        'g_mix': 1.0 + nrm(11, (DEPTH, D_MODEL), 0.02),
        'w_in': nrm(12, (DEPTH, D_MODEL, D_IN), D_MODEL ** -0.5),
        'b_fox_f': FOX_BIAS_INIT + nrm(13, (DEPTH, FOX_HEADS), 0.1),
        'w_gla_a2': nrm(14, (DEPTH, GLA_RANK, GLA_KW), GLA_RANK ** -0.5),
        'b_gla_a': nrm(15, (DEPTH, GLA_KW), 0.1),
        'g_gla': 1.0 + nrm(16, (DEPTH, GLA_VW), 0.02),
        'g_ret': 1.0 + nrm(17, (DEPTH, RET_W), 0.02),
        'w_mem_kv': nrm(18, (DEPTH, D_MODEL, 2 * MEM_W), D_MODEL ** -0.5),
        'w_branch': nrm(19, (DEPTH, N_BRANCH, BRANCH_W, D_MODEL), BRANCH_W ** -0.5),
        'w_out': nrm(20, (DEPTH, D_MODEL, D_MODEL), D_MODEL ** -0.5),
        'g_mlp': 1.0 + nrm(21, (DEPTH, D_MODEL), 0.02),
        'w_ff1': nrm(22, (DEPTH, D_MODEL, D_FF), D_MODEL ** -0.5),
        'w_ff2': nrm(23, (DEPTH, D_FF, D_MODEL), D_FF ** -0.5),
        'g_final': 1.0 + nrm(24, (D_MODEL,), 0.02),
    }


def reference(x_prompt, x_sample, cache_fox_k, cache_fox_v, cache_fox_logf, state_gla, state_ret,
              cache_mem_k, cache_mem_v, page_table, mem_prompt, g_mix, w_in, b_fox_f, w_gla_a2,
              b_gla_a, g_gla, g_ret, w_mem_kv, w_branch, w_out, g_mlp, w_ff1, w_ff2, g_final):
    n_prompt = x_prompt.shape[0]
    past = page_table.shape[1] * PAGE_SIZE
    pos_prompt = jnp.arange(x_prompt.shape[1])
    pos_sample = past + jnp.arange(x_sample.shape[1])
    xp, xs = x_prompt, x_sample
    fk_p, fv_p, fl_p, sg_p, sr_p, mk_p, mv_p = [], [], [], [], [], [], []
    fk_s, fv_s, fl_s, sg_s, sr_s = [], [], [], [], []
    for l in range(DEPTH):
        w = (g_mix[l], w_in[l], b_fox_f[l], w_gla_a2[l], b_gla_a[l], g_gla[l], g_ret[l],
             w_branch[l], w_out[l], g_mlp[l], w_ff1[l], w_ff2[l])
        mk, mv = jnp.split(mem_prompt @ w_mem_kv[l], 2, axis=-1)
        mk = mk.reshape(n_prompt, N_MEM, MEM_HEADS, MEM_DH)
        mv = mv.reshape(n_prompt, N_MEM, MEM_HEADS, MEM_DH)
        gla0 = jnp.zeros((n_prompt, GLA_HEADS, GLA_DK, GLA_DV), jnp.float32)
        ret0 = jnp.zeros((n_prompt, RET_HEADS, RET_DH, RET_DH), jnp.float32)
        xp, (k_new, v_new, lf_new), s_gla, s_ret = layer(xp, pos_prompt, None, gla0, ret0, mk, mv, *w)
        fk_p.append(k_new); fv_p.append(v_new); fl_p.append(lf_new)
        sg_p.append(s_gla); sr_p.append(s_ret); mk_p.append(mk); mv_p.append(mv)
        fox_past = (gather_pages(cache_fox_k[l], page_table), gather_pages(cache_fox_v[l], page_table),
                    gather_pages(cache_fox_logf[l], page_table))
        xs, (k_new, v_new, lf_new), s_gla, s_ret = layer(xs, pos_sample, fox_past, state_gla[l], state_ret[l],
                                                         cache_mem_k[l], cache_mem_v[l], *w)
        fk_s.append(k_new); fv_s.append(v_new); fl_s.append(lf_new)
        sg_s.append(s_gla); sr_s.append(s_ret)
    y_prompt = rmsnorm(xp, g_final)
    y_sample = rmsnorm(xs, g_final)
    return (y_prompt, y_sample,
            jnp.stack(fk_p), jnp.stack(fv_p), jnp.stack(fl_p), jnp.stack(sg_p), jnp.stack(sr_p),
            jnp.stack(mk_p), jnp.stack(mv_p),
            jnp.stack(fk_s), jnp.stack(fv_s), jnp.stack(fl_s), jnp.stack(sg_s), jnp.stack(sr_s))
```

```python
import functools

import numpy as np
import jax
import jax.numpy as jnp
from jax import lax
from jax.experimental import pallas as pl
from jax.experimental.pallas import tpu as pltpu

F32 = jnp.float32
BF16 = jnp.bfloat16

D_MODEL = 2048
FOX_HEADS = 8
FOX_DH = D_MODEL // 16
FOX_W = FOX_HEADS * FOX_DH
GLA_HEADS = 4
GLA_DK = D_MODEL // 16
GLA_DV = D_MODEL // 8
GLA_KW = GLA_HEADS * GLA_DK
GLA_VW = GLA_HEADS * GLA_DV
GLA_RANK = 16
GLA_TAU = 16.0
RET_HEADS = 4
RET_DH = D_MODEL // 8
RET_W = RET_HEADS * RET_DH
MEM_HEADS = 4
MEM_DH = D_MODEL // 8
MEM_W = MEM_HEADS * MEM_DH
N_BRANCH = 4
BRANCH_W = D_MODEL // 2
D_FF = 4 * D_MODEL
PAGE_SIZE = 128
ROPE_BASE = 10000.0
EPS = 1e-6

LANES = 128
SUBLANES = 8
BF16_ROWS = 16
VMEM_LIMIT_BYTES = 56 * 1024 * 1024

OFF_FQ = 0
OFF_FK = OFF_FQ + FOX_W
OFF_FV = OFF_FK + FOX_W
OFF_GQ = OFF_FV + FOX_W
OFF_GK = OFF_GQ + GLA_KW
OFF_GV = OFF_GK + GLA_KW
OFF_GR = OFF_GV + GLA_VW
OFF_RQ = OFF_GR + GLA_VW
OFF_RK = OFF_RQ + RET_W
OFF_RV = OFF_RK + RET_W
OFF_RG = OFF_RV + RET_W
OFF_MQ = OFF_RG + RET_W
OFF_GT = OFF_MQ + MEM_W
OFF_SM = OFF_GT + N_BRANCH * D_MODEL
SM_FF = 0
SM_GA = FOX_HEADS
N_PACK = OFF_SM + LANES

GLA_CHUNK = 64
GLA_SUB = 16
GLA_SUB_LOG2 = 4
FOX_DH_LOG2 = 7
assert GLA_SUB == 1 << GLA_SUB_LOG2 and FOX_DH == 1 << FOX_DH_LOG2 and LANES & (LANES - 1) == 0
RET_CHUNK = 256


def _cparams(sem):
    return pltpu.CompilerParams(dimension_semantics=sem, vmem_limit_bytes=VMEM_LIMIT_BYTES)


def _log_sigmoid(x):
    return jnp.minimum(x, 0.0) - jnp.log1p(jnp.exp(-jnp.abs(x)))


def _sigmoid(x):
    return 1.0 / (1.0 + jnp.exp(-x))


def _dot(a, b):
    return jnp.dot(a, b, preferred_element_type=F32)


def _dot_nt(a, b):
    return lax.dot_general(a, b, (((1,), (1,)), ((), ())), preferred_element_type=F32)


def _dot_tn(a, b):
    return lax.dot_general(a, b, (((0,), (0,)), ((), ())), preferred_element_type=F32)


def _dot01(m01, x):
    hi = x.astype(BF16)
    r1 = x - hi.astype(F32)
    mid = r1.astype(BF16)
    lo = (r1 - mid.astype(F32)).astype(BF16)
    return _dot(m01, hi) + _dot(m01, mid) + _dot(m01, lo)


def _dot01_right(x, m01):
    hi = x.astype(BF16)
    r1 = x - hi.astype(F32)
    mid = r1.astype(BF16)
    lo = (r1 - mid.astype(F32)).astype(BF16)
    return _dot(hi, m01) + _dot(mid, m01) + _dot(lo, m01)


def _row_to_cols(row):
    return jnp.broadcast_to(row, (LANES, LANES)).T


def _head_norm_gate(o, gain, gate):
    mu = jnp.mean(o, axis=1, keepdims=True)
    d = o - mu
    var = jnp.mean(d * d, axis=1, keepdims=True)
    y = d * lax.rsqrt(var + EPS) * gain
    return y * (gate * _sigmoid(gate))


def _mm_body(*refs, norm, act, has_res, stage):
    it = iter(refs)
    x_ref = next(it)
    g_ref = next(it) if norm else None
    w_ref = next(it)
    r_ref = next(it) if has_res else None
    o_ref = next(it)
    h_ref = next(it) if stage else None
    if stage:
        @pl.when(pl.program_id(1) == 0)
        def _():
            x = x_ref[...].astype(F32)
            if norm:
                ms = jnp.mean(x * x, axis=-1, keepdims=True)
                x = x * lax.rsqrt(ms + EPS) * g_ref[...]
            h_ref[...] = x.astype(BF16)

        h = h_ref[...]
    else:
        h = x_ref[...]
    acc = _dot(h, w_ref[...])
    if act == "relu2":
        acc = jnp.square(jnp.maximum(acc, 0.0))
    if has_res:
        acc = acc + r_ref[...]
    o_ref[...] = acc.astype(o_ref.dtype)


def _pick_tile(n, candidates):
    for c in candidates:
        if n % c == 0:
            return c
    return n


def _matmul(x, w, *, g=None, res=None, act=None, out_dtype=F32, name="proj"):
    m, k = x.shape
    n = w.shape[1]
    norm = g is not None
    stage = norm or x.dtype != BF16
    if m >= 1024:
        tm = 1024 if k <= 2048 else 512
    else:
        tm = m
    tn = _pick_tile(n, (1152, 1024, 512))
    if k > 2048:
        tn = min(tn, 512)
    assert m % tm == 0 and n % tn == 0 and tm % BF16_ROWS == 0
    in_specs = [pl.BlockSpec((tm, k), lambda i, j: (i, 0))]
    args = [x]
    if norm:
        in_specs.append(pl.BlockSpec((1, k), lambda i, j: (0, 0)))
        args.append(g.reshape(1, k).astype(F32))
    in_specs.append(pl.BlockSpec((k, tn), lambda i, j: (0, j)))
    args.append(w)
    if res is not None:
        in_specs.append(pl.BlockSpec((tm, tn), lambda i, j: (i, j)))
        args.append(res)
    scratch = [pltpu.VMEM((tm, k), BF16)] if stage else []
    return pl.pallas_call(
        functools.partial(_mm_body, norm=norm, act=act, has_res=res is not None, stage=stage),
        grid=(m // tm, n // tn),
        in_specs=in_specs,
        out_specs=pl.BlockSpec((tm, tn), lambda i, j: (i, j)),
        out_shape=jax.ShapeDtypeStruct((m, n), out_dtype),
        scratch_shapes=scratch,
        compiler_params=_cparams(("parallel", "arbitrary")),
        name=name,
    )(*args)


def _rmsnorm_body(x_ref, g_ref, o_ref):
    x = x_ref[...]
    ms = jnp.mean(x * x, axis=-1, keepdims=True)
    o_ref[...] = x * lax.rsqrt(ms + EPS) * g_ref[...]


def _rmsnorm(x, g):
    m, k = x.shape
    tm = _pick_tile(m, (512,))
    return pl.pallas_call(
        _rmsnorm_body,
        grid=(m // tm,),
        in_specs=[pl.BlockSpec((tm, k), lambda i: (i, 0)), pl.BlockSpec((1, k), lambda i: (0, 0))],
        out_specs=pl.BlockSpec((tm, k), lambda i: (i, 0)),
        out_shape=jax.ShapeDtypeStruct((m, k), F32),
        compiler_params=_cparams(("parallel",)),
        name="final_norm",
    )(x, g.reshape(1, k))


def _fox_gate_body(x_ref, b_ref, lf_ref, c_ref):
    lf = _log_sigmoid(x_ref[...] + b_ref[...])
    lf_ref[...] = lf
    t = lf.shape[1]
    lane = lax.broadcasted_iota(jnp.int32, lf.shape, 1)
    c = lf
    shift = 1
    while shift < t:
        c = c + jnp.where(lane >= shift, pltpu.roll(c, shift, axis=1), 0.0)
        shift *= 2
    c_ref[...] = c


def _fox_gate(ff_t, bias):
    b, h, t = ff_t.shape
    spec = pl.BlockSpec((None, h, t), lambda i: (i, 0, 0))
    return pl.pallas_call(
        _fox_gate_body,
        grid=(b,),
        in_specs=[spec, pl.BlockSpec((h, 1), lambda i: (0, 0))],
        out_specs=[spec, spec],
        out_shape=[jax.ShapeDtypeStruct((b, h, t), F32)] * 2,
        compiler_params=_cparams(("parallel",)),
        name="fox_gate",
    )(ff_t, bias.reshape(h, 1))


def _fox_attn_body(q_ref, k_ref, v_ref, cq_ref, ck_ref, o_ref, m_sc, l_sc, acc_sc, *, tile):
    qi = pl.program_id(1)
    ki = pl.program_id(2)
    nk = pl.num_programs(2)
    scale = FOX_DH ** -0.5

    @pl.when(ki == 0)
    def _():
        m_sc[...] = jnp.full(m_sc.shape, -jnp.inf, F32)
        l_sc[...] = jnp.zeros(l_sc.shape, F32)
        acc_sc[...] = jnp.zeros(acc_sc.shape, F32)

    @pl.when(ki <= qi)
    def _():
        row = lax.broadcasted_iota(jnp.int32, (tile, tile), 0) + qi * tile
        col = lax.broadcasted_iota(jnp.int32, (tile, tile), 1) + ki * tile
        visible = col <= row
        cq = cq_ref[...]
        ck = ck_ref[...]
        for h in range(FOX_HEADS):
            sl = slice(h * FOX_DH, (h + 1) * FOX_DH)
            q = q_ref[:, sl].astype(BF16)
            k = k_ref[:, sl].astype(BF16)
            s = _dot_nt(q, k) * scale + (cq[:, h:h + 1] - ck[h:h + 1, :])
            s = jnp.where(visible, s, -jnp.inf)
            m_prev = m_sc[:, h:h + 1]
            m_new = jnp.maximum(m_prev, jnp.max(s, axis=1, keepdims=True))
            alpha = jnp.exp(m_prev - m_new)
            p = jnp.exp(s - m_new)
            l_sc[:, h:h + 1] = alpha * l_sc[:, h:h + 1] + jnp.sum(p, axis=1, keepdims=True)
            acc_sc[:, sl] = alpha * acc_sc[:, sl] + _dot(p.astype(BF16), v_ref[:, sl].astype(BF16))
            m_sc[:, h:h + 1] = m_new

    @pl.when(ki == nk - 1)
    def _():
        for h in range(FOX_HEADS):
            sl = slice(h * FOX_DH, (h + 1) * FOX_DH)
            o_ref[:, sl] = (acc_sc[:, sl] / l_sc[:, h:h + 1]).astype(o_ref.dtype)


def _fox_attn(z3, c_col, c_row):
    b, t, _ = z3.shape
    tile = _pick_tile(t, (512, 256, 128))
    n = t // tile
    qcol, kcol, vcol = OFF_FQ // FOX_W, OFF_FK // FOX_W, OFF_FV // FOX_W
    return pl.pallas_call(
        functools.partial(_fox_attn_body, tile=tile),
        grid=(b, n, n),
        in_specs=[
            pl.BlockSpec((None, tile, FOX_W), lambda bi, qi, ki: (bi, qi, qcol)),
            pl.BlockSpec((None, tile, FOX_W), lambda bi, qi, ki: (bi, jnp.minimum(ki, qi), kcol)),
            pl.BlockSpec((None, tile, FOX_W), lambda bi, qi, ki: (bi, jnp.minimum(ki, qi), vcol)),
            pl.BlockSpec((None, tile, FOX_HEADS), lambda bi, qi, ki: (bi, qi, 0)),
            pl.BlockSpec((None, FOX_HEADS, tile), lambda bi, qi, ki: (bi, 0, jnp.minimum(ki, qi))),
        ],
        out_specs=pl.BlockSpec((None, tile, FOX_W), lambda bi, qi, ki: (bi, qi, 0)),
        out_shape=jax.ShapeDtypeStruct((b, t, FOX_W), BF16),
        scratch_shapes=[
            pltpu.VMEM((tile, LANES), F32),
            pltpu.VMEM((tile, LANES), F32),
            pltpu.VMEM((tile, FOX_W), F32),
        ],
        compiler_params=_cparams(("parallel", "parallel", "arbitrary")),
        name="fox_attn",
    )(z3, z3, z3, c_col, c_row)


def _mem_attn_body(q_ref, k_ref, v_ref, o_ref, *, rows):
    scale = MEM_DH ** -0.5
    for h in range(MEM_HEADS):
        sl = slice(h * MEM_DH, (h + 1) * MEM_DH)
        q = q_ref[:, sl]
        if rows < SUBLANES:
            q = jnp.broadcast_to(q, (SUBLANES, MEM_DH))
        s = _dot_nt(q.astype(BF16), k_ref[:, sl].astype(BF16)) * scale
        m = jnp.max(s, axis=1, keepdims=True)
        p = jnp.exp(s - m)
        p = p / jnp.sum(p, axis=1, keepdims=True)
        o = _dot(p.astype(BF16), v_ref[:, sl].astype(BF16))
        o_ref[:, sl] = o[:rows].astype(o_ref.dtype)


def _mem_attn(z3, k4, v4, *, layer, kcol, vcol, out_dtype):
    b, t, _ = z3.shape
    n_mem = k4.shape[2]
    tq = _pick_tile(t, (512, 256, 128))
    qcol = OFF_MQ // MEM_W
    return pl.pallas_call(
        functools.partial(_mem_attn_body, rows=tq),
        grid=(b, t // tq),
        in_specs=[
            pl.BlockSpec((None, tq, MEM_W), lambda bi, qi: (bi, qi, qcol)),
            pl.BlockSpec((None, None, n_mem, MEM_W), lambda bi, qi: (layer, bi, 0, kcol)),
            pl.BlockSpec((None, None, n_mem, MEM_W), lambda bi, qi: (layer, bi, 0, vcol)),
        ],
        out_specs=pl.BlockSpec((None, tq, MEM_W), lambda bi, qi: (bi, qi, 0)),
        out_shape=jax.ShapeDtypeStruct((b, t, MEM_W), out_dtype),
        compiler_params=_cparams(("parallel", "parallel")),
        name="mem_attn",
    )(z3, k4, v4)


def _rotary(x, cos, sin):
    half = x.shape[1] // 2
    x1, x2 = x[:, :half], x[:, half:]
    return jnp.concatenate([x1 * cos - x2 * sin, x1 * sin + x2 * cos], axis=1)


def _ret_body(q_ref, k_ref, v_ref, g_ref, cos_ref, sin_ref, lg_ref, gn_ref, o_ref, s_ref, st_sc, *, chunk):
    c = pl.program_id(2)
    nc = pl.num_programs(2)

    @pl.when(c == 0)
    def _():
        st_sc[...] = jnp.zeros(st_sc.shape, F32)

    lg = lg_ref[:, :1]
    cos, sin = cos_ref[...], sin_ref[...]
    q = _rotary(q_ref[...], cos, sin)
    k = _rotary(k_ref[...], cos, sin) * (RET_DH ** -0.5)
    v = v_ref[...].astype(BF16)
    row = lax.broadcasted_iota(jnp.int32, (chunk, chunk), 0)
    col = lax.broadcasted_iota(jnp.int32, (chunk, chunk), 1)
    decay = jnp.where(col <= row, jnp.exp((row - col).astype(F32) * lg), 0.0)
    t = lax.broadcasted_iota(jnp.int32, (chunk, 1), 0).astype(F32)
    a = _dot_nt(q.astype(BF16), k.astype(BF16)) * decay
    s0 = st_sc[...]
    o = _dot(a.astype(BF16), v) + jnp.exp((t + 1.0) * lg) * _dot(q.astype(BF16), s0.astype(BF16))
    k_end = k * jnp.exp((chunk - 1.0 - t) * lg)
    s_new = jnp.exp(chunk * lg) * s0 + _dot_tn(k_end.astype(BF16), v)
    st_sc[...] = s_new
    o_ref[...] = _head_norm_gate(o, gn_ref[...], g_ref[...]).astype(o_ref.dtype)

    @pl.when(c == nc - 1)
    def _():
        s_ref[...] = s_new


def _ret_prompt(z3, cos, sin, lg_tab, g_ret):
    b, t, _ = z3.shape
    chunk = _pick_tile(t, (RET_CHUNK, 128, 64))
    cb = lambda off: off // RET_DH
    qcol, kcol, vcol, gcol = cb(OFF_RQ), cb(OFF_RK), cb(OFF_RV), cb(OFF_RG)
    zspec = lambda col: pl.BlockSpec((None, chunk, RET_DH), lambda bi, h, c: (bi, c, col + h))
    return pl.pallas_call(
        functools.partial(_ret_body, chunk=chunk),
        grid=(b, RET_HEADS, t // chunk),
        in_specs=[
            zspec(qcol), zspec(kcol), zspec(vcol), zspec(gcol),
            pl.BlockSpec((chunk, RET_DH // 2), lambda bi, h, c: (c, 0)),
            pl.BlockSpec((chunk, RET_DH // 2), lambda bi, h, c: (c, 0)),
            pl.BlockSpec((None, 1, LANES), lambda bi, h, c: (h, 0, 0)),
            pl.BlockSpec((1, RET_DH), lambda bi, h, c: (0, h)),
        ],
        out_specs=[
            pl.BlockSpec((None, chunk, RET_DH), lambda bi, h, c: (bi, c, h)),
            pl.BlockSpec((None, None, RET_DH, RET_DH), lambda bi, h, c: (bi, h, 0, 0)),
        ],
        out_shape=[
            jax.ShapeDtypeStruct((b, t, RET_W), BF16),
            jax.ShapeDtypeStruct((b, RET_HEADS, RET_DH, RET_DH), F32),
        ],
        scratch_shapes=[pltpu.VMEM((RET_DH, RET_DH), F32)],
        compiler_params=_cparams(("parallel", "parallel", "arbitrary")),
        name="ret_prompt",
    )(z3, z3, z3, z3, cos, sin, lg_tab, g_ret.reshape(1, RET_W))


def _gla_loga(zs, w2_ref, ba_ref):
    pre = _dot(zs.astype(BF16), w2_ref[...]) + ba_ref[...]
    return _log_sigmoid(pre) * (1.0 / GLA_TAU)


def _gla_body(q_ref, k_ref, v_ref, gr_ref, zs_ref, w2_ref, ba_ref, gn_ref, tri_ref, hsum_ref,
              o_ref, s_ref, st_sc, *, chunk):
    c = pl.program_id(1)
    nc = pl.num_programs(1)
    n_sub = chunk // GLA_SUB

    @pl.when(c == 0)
    def _():
        st_sc[...] = jnp.zeros(st_sc.shape, F32)

    loga = _gla_loga(zs_ref[...], w2_ref, ba_ref)
    b = _dot01(tri_ref[...], loga)
    q = q_ref[...] * (GLA_DK ** -0.5)
    k = k_ref[...]
    row = lax.broadcasted_iota(jnp.int32, (chunk, GLA_KW), 0)
    b_last = b[chunk - 1:chunk, :]
    beta = jnp.zeros_like(b)
    for i in range(1, n_sub):
        beta = jnp.where(row >= i * GLA_SUB, b[i * GLA_SUB - 1:i * GLA_SUB, :], beta)
    q_rel = q * jnp.exp(b - beta)
    q_abs = q * jnp.exp(b)
    k_end = k * jnp.exp(b_last - b)
    k_rel = []
    for i in range(1, n_sub):
        bi = b[i * GLA_SUB - 1:i * GLA_SUB, :]
        k_rel.append(jnp.where(row < i * GLA_SUB, k * jnp.exp(jnp.minimum(bi - b, 0.0)), 0.0).astype(BF16))

    rowb = lax.broadcasted_iota(jnp.int32, (chunk, GLA_HEADS * LANES), 0)
    colb = lax.broadcasted_iota(jnp.int32, (chunk, GLA_HEADS * LANES), 1) & (LANES - 1)
    rsub = rowb & (GLA_SUB - 1)
    band = jnp.zeros((chunk, GLA_HEADS * LANES), F32)
    hsum = hsum_ref[...]
    for delta in range(GLA_SUB):
        if delta == 0:
            kd, bd = k, b
        else:
            kd = pltpu.roll(k, delta, axis=0)
            bd = pltpu.roll(b, delta, axis=0)
        prod = q * kd * jnp.exp(jnp.minimum(b - bd, 0.0))
        ad = _dot(prod.astype(BF16), hsum)
        keep = jnp.logical_and(colb == rowb - delta, rsub >= delta)
        band = band + jnp.where(keep, ad, 0.0)

    sub_of_row = row[:, :GLA_DK] >> GLA_SUB_LOG2
    for h in range(GLA_HEADS):
        ks = slice(h * GLA_DK, (h + 1) * GLA_DK)
        vs = slice(h * GLA_DV, (h + 1) * GLA_DV)
        v = v_ref[:, vs].astype(BF16)
        lhs = jnp.concatenate(
            [jnp.where(sub_of_row == i, q_rel[:, ks], 0.0) for i in range(1, n_sub)], axis=1).astype(BF16)
        rhs = jnp.concatenate([kr[:, ks] for kr in k_rel], axis=1)
        a = _dot_nt(lhs, rhs) + band[:, h * LANES:h * LANES + chunk]
        st = st_sc[h]
        o = _dot(a.astype(BF16), v) + _dot_nt(q_abs[:, ks].astype(BF16), st.astype(BF16))
        st_new = st * jnp.exp(b_last[:, ks]) + _dot_tn(v, k_end[:, ks].astype(BF16))
        st_sc[h] = st_new
        o_ref[:, vs] = _head_norm_gate(o, gn_ref[:, vs], gr_ref[:, vs]).astype(o_ref.dtype)

        @pl.when(c == nc - 1)
        def _():
            s_ref[h] = st_new.T


def _gla_consts(chunk):
    r = np.arange(chunk)
    tri = (r[None, :] <= r[:, None]).astype(np.float32)
    lane = np.arange(GLA_KW)
    out = np.arange(GLA_HEADS * LANES)
    hsum = (lane[:, None] // GLA_DK == out[None, :] // LANES).astype(np.float32)
    return jnp.asarray(tri, BF16), jnp.asarray(hsum, BF16)


def _gla_prompt(z3, w2p, b_a, g_gla):
    b, t, _ = z3.shape
    chunk = GLA_CHUNK
    assert t % chunk == 0
    tri, hsum = _gla_consts(chunk)
    full = lambda shape: pl.BlockSpec(shape, lambda bi, c: (0,) * len(shape))
    return pl.pallas_call(
        functools.partial(_gla_body, chunk=chunk),
        grid=(b, t // chunk),
        in_specs=[
            pl.BlockSpec((None, chunk, GLA_KW), lambda bi, c: (bi, c, OFF_GQ // GLA_KW)),
            pl.BlockSpec((None, chunk, GLA_KW), lambda bi, c: (bi, c, OFF_GK // GLA_KW)),
            pl.BlockSpec((None, chunk, GLA_VW), lambda bi, c: (bi, c, OFF_GV // GLA_VW)),
            pl.BlockSpec((None, chunk, GLA_VW), lambda bi, c: (bi, c, OFF_GR // GLA_VW)),
            pl.BlockSpec((None, chunk, LANES), lambda bi, c: (bi, c, OFF_SM // LANES)),
            full((LANES, GLA_KW)), full((1, GLA_KW)), full((1, GLA_VW)),
            full((chunk, chunk)), full((GLA_KW, GLA_HEADS * LANES)),
        ],
        out_specs=[
            pl.BlockSpec((None, chunk, GLA_VW), lambda bi, c: (bi, c, 0)),
            pl.BlockSpec((None, GLA_HEADS, GLA_DK, GLA_DV), lambda bi, c: (bi, 0, 0, 0)),
        ],
        out_shape=[
            jax.ShapeDtypeStruct((b, t, GLA_VW), BF16),
            jax.ShapeDtypeStruct((b, GLA_HEADS, GLA_DK, GLA_DV), F32),
        ],
        scratch_shapes=[pltpu.VMEM((GLA_HEADS, GLA_DV, GLA_DK), F32)],
        compiler_params=_cparams(("parallel", "arbitrary")),
        name="gla_prompt",
    )(z3, z3, z3, z3, z3, w2p, b_a.reshape(1, GLA_KW), g_gla.reshape(1, GLA_VW), tri, hsum)


def _gla_step_body(q_ref, k_ref, v_ref, gr_ref, zs_ref, w2_ref, ba_ref, gn_ref, s0_ref, o_ref, s_ref):
    zs = jnp.broadcast_to(zs_ref[...], (BF16_ROWS, LANES))
    loga = _gla_loga(zs, w2_ref, ba_ref)[:1]
    q = q_ref[...] * (GLA_DK ** -0.5)
    k = k_ref[...]
    for h in range(GLA_HEADS):
        ks = slice(h * GLA_DK, (h + 1) * GLA_DK)
        vs = slice(h * GLA_DV, (h + 1) * GLA_DV)
        widen = lambda m: jnp.concatenate([m] * (GLA_DV // LANES), axis=1)
        a_c = widen(_row_to_cols(jnp.exp(loga[:, ks])))
        k_c = widen(_row_to_cols(k[:, ks]))
        q_c = widen(_row_to_cols(q[:, ks]))
        s_new = a_c * s0_ref[h] + k_c * v_ref[:, vs]
        s_ref[h] = s_new
        o = jnp.sum(q_c * s_new, axis=0, keepdims=True)
        o_ref[:, vs] = _head_norm_gate(o, gn_ref[:, vs], gr_ref[:, vs])


def _gla_step(zs3, w2p, b_a, g_gla, state, layer):
    b = zs3.shape[0]
    full = lambda shape: pl.BlockSpec(shape, lambda bi: (0,) * len(shape))
    return pl.pallas_call(
        _gla_step_body,
        grid=(b,),
        in_specs=[
            pl.BlockSpec((None, 1, GLA_KW), lambda bi: (bi, 0, OFF_GQ // GLA_KW)),
            pl.BlockSpec((None, 1, GLA_KW), lambda bi: (bi, 0, OFF_GK // GLA_KW)),
            pl.BlockSpec((None, 1, GLA_VW), lambda bi: (bi, 0, OFF_GV // GLA_VW)),
            pl.BlockSpec((None, 1, GLA_VW), lambda bi: (bi, 0, OFF_GR // GLA_VW)),
            pl.BlockSpec((None, 1, LANES), lambda bi: (bi, 0, OFF_SM // LANES)),
            full((LANES, GLA_KW)), full((1, GLA_KW)), full((1, GLA_VW)),
            pl.BlockSpec((None, None, GLA_HEADS, GLA_DK, GLA_DV), lambda bi: (layer, bi, 0, 0, 0)),
        ],
        out_specs=[
            pl.BlockSpec((None, 1, GLA_VW), lambda bi: (bi, 0, 0)),
            pl.BlockSpec((None, GLA_HEADS, GLA_DK, GLA_DV), lambda bi: (bi, 0, 0, 0)),
        ],
        out_shape=[
            jax.ShapeDtypeStruct((b, 1, GLA_VW), F32),
            jax.ShapeDtypeStruct((b, GLA_HEADS, GLA_DK, GLA_DV), F32),
        ],
        compiler_params=_cparams(("parallel",)),
        name="gla_step",
    )(zs3, zs3, zs3, zs3, zs3, w2p, b_a.reshape(1, GLA_KW), g_gla.reshape(1, GLA_VW), state)


def _ret_step_body(q_ref, k_ref, v_ref, g_ref, cos_ref, sin_ref, lg_ref, gn_ref, s0_ref, o_ref, s_ref):
    cos, sin = cos_ref[...], sin_ref[...]
    half = RET_DH // 2
    for h in range(RET_HEADS):
        sl = slice(h * RET_DH, (h + 1) * RET_DH)
        q = _rotary(q_ref[:, sl], cos, sin)
        k = _rotary(k_ref[:, sl], cos, sin) * (RET_DH ** -0.5)
        cols = lambda r: jnp.concatenate(
            [jnp.concatenate([_row_to_cols(r[:, i * half:(i + 1) * half])] * 2, axis=1) for i in range(2)], axis=0)
        gamma = jnp.exp(lg_ref[h][:, :1])
        s_new = gamma * s0_ref[h] + cols(k) * v_ref[:, sl]
        s_ref[h] = s_new
        o = jnp.sum(cols(q) * s_new, axis=0, keepdims=True)
        o_ref[:, sl] = _head_norm_gate(o, gn_ref[:, sl], g_ref[:, sl])


def _ret_step(zs3, cos, sin, lg_tab, g_ret, state, layer):
    b = zs3.shape[0]
    full = lambda shape: pl.BlockSpec(shape, lambda bi: (0,) * len(shape))
    zspec = lambda off: pl.BlockSpec((None, 1, RET_W), lambda bi: (bi, 0, off // RET_W))
    return pl.pallas_call(
        _ret_step_body,
        grid=(b,),
        in_specs=[
            zspec(OFF_RQ), zspec(OFF_RK), zspec(OFF_RV), zspec(OFF_RG),
            full((1, RET_DH // 2)), full((1, RET_DH // 2)), full((RET_HEADS, 1, LANES)), full((1, RET_W)),
            pl.BlockSpec((None, None, RET_HEADS, RET_DH, RET_DH), lambda bi: (layer, bi, 0, 0, 0)),
        ],
        out_specs=[
            pl.BlockSpec((None, 1, RET_W), lambda bi: (bi, 0, 0)),
            pl.BlockSpec((None, RET_HEADS, RET_DH, RET_DH), lambda bi: (bi, 0, 0, 0)),
        ],
        out_shape=[
            jax.ShapeDtypeStruct((b, 1, RET_W), F32),
            jax.ShapeDtypeStruct((b, RET_HEADS, RET_DH, RET_DH), F32),
        ],
        compiler_params=_cparams(("parallel",)),
        name="ret_step",
    )(zs3, zs3, zs3, zs3, cos, sin, lg_tab, g_ret.reshape(1, RET_W), state)


def _fox_decode_body(pt_ref, q_ref, kn_ref, vn_ref, ff_ref, bf_ref, kp_ref, vp_ref, lp_ref, suf_ref,
                     o_ref, lf_ref, m_sc, l_sc, tail_sc, acc_sc):
    del pt_ref
    p = pl.program_id(1)
    n_pages = pl.num_programs(1)
    scale = FOX_DH ** -0.5
    hrow = lax.broadcasted_iota(jnp.int32, (FOX_HEADS, FOX_W), 0)
    hcol = lax.broadcasted_iota(jnp.int32, (FOX_HEADS, FOX_W), 1) >> FOX_DH_LOG2
    own = hrow == hcol
    q_bd = jnp.where(own, jnp.broadcast_to(q_ref[...], (FOX_HEADS, FOX_W)), 0.0)

    @pl.when(p == 0)
    def _():
        lf_new = _log_sigmoid(ff_ref[...] + bf_ref[...])
        lf_ref[...] = lf_new
        tail_sc[...] = jnp.broadcast_to(lf_new, tail_sc.shape)
        s_new = jnp.sum(q_bd * kn_ref[...], axis=1, keepdims=True) * scale
        m_sc[...] = jnp.broadcast_to(s_new, m_sc.shape)
        l_sc[...] = jnp.ones(l_sc.shape, F32)
        acc_sc[...] = jnp.where(own, jnp.broadcast_to(vn_ref[...], (FOX_HEADS, FOX_W)), 0.0)

    lp = lp_ref[...]
    tail = tail_sc[:, :1]
    bias = tail + _dot01_right(lp, suf_ref[...])
    s = _dot_nt(q_bd.astype(BF16), kp_ref[...].astype(BF16)) * scale + bias
    m_prev = m_sc[:, :1]
    m_new = jnp.maximum(m_prev, jnp.max(s, axis=1, keepdims=True))
    alpha = jnp.exp(m_prev - m_new)
    pr = jnp.exp(s - m_new)
    l_sc[...] = jnp.broadcast_to(alpha * l_sc[:, :1] + jnp.sum(pr, axis=1, keepdims=True), l_sc.shape)
    pv = _dot(pr.astype(BF16), vp_ref[...].astype(BF16))
    acc_sc[...] = alpha * acc_sc[...] + jnp.where(own, pv, 0.0)
    m_sc[...] = jnp.broadcast_to(m_new, m_sc.shape)
    tail_sc[...] = jnp.broadcast_to(tail + jnp.sum(lp, axis=1, keepdims=True), tail_sc.shape)

    @pl.when(p == n_pages - 1)
    def _():
        o_ref[...] = jnp.sum(acc_sc[...] / l_sc[:, :1], axis=0, keepdims=True)


def _fox_decode(page_table, zs3, ff_col, bf_col, k_pool, v_pool, lf_pool_t, layer):
    b, n_pages = page_table.shape
    r = np.arange(PAGE_SIZE)
    suffix = jnp.asarray((r[:, None] > r[None, :]).astype(np.float32), BF16)
    page = lambda bi, p, pt: (layer, pt[bi, n_pages - 1 - p], 0, 0)
    zspec = lambda off: pl.BlockSpec((None, 1, FOX_W), lambda bi, p, pt: (bi, 0, off // FOX_W))
    grid_spec = pltpu.PrefetchScalarGridSpec(
        num_scalar_prefetch=1,
        grid=(b, n_pages),
        in_specs=[
            zspec(OFF_FQ), zspec(OFF_FK), zspec(OFF_FV),
            pl.BlockSpec((None, FOX_HEADS, 1), lambda bi, p, pt: (bi, 0, 0)),
            pl.BlockSpec((FOX_HEADS, 1), lambda bi, p, pt: (0, 0)),
            pl.BlockSpec((None, None, PAGE_SIZE, FOX_W), page),
            pl.BlockSpec((None, None, PAGE_SIZE, FOX_W), page),
            pl.BlockSpec((None, None, FOX_HEADS, PAGE_SIZE), page),
            pl.BlockSpec((PAGE_SIZE, PAGE_SIZE), lambda bi, p, pt: (0, 0)),
        ],
        out_specs=[
            pl.BlockSpec((None, 1, FOX_W), lambda bi, p, pt: (bi, 0, 0)),
            pl.BlockSpec((None, FOX_HEADS, 1), lambda bi, p, pt: (bi, 0, 0)),
        ],
        scratch_shapes=[
            pltpu.VMEM((FOX_HEADS, LANES), F32),
            pltpu.VMEM((FOX_HEADS, LANES), F32),
            pltpu.VMEM((FOX_HEADS, LANES), F32),
            pltpu.VMEM((FOX_HEADS, FOX_W), F32),
        ],
    )
    return pl.pallas_call(
        _fox_decode_body,
        grid_spec=grid_spec,
        out_shape=[
            jax.ShapeDtypeStruct((b, 1, FOX_W), F32),
            jax.ShapeDtypeStruct((b, FOX_HEADS, 1), F32),
        ],
        compiler_params=_cparams(("parallel", "arbitrary")),
        name="fox_decode",
    )(page_table, zs3, zs3, zs3, ff_col, bf_col, k_pool, v_pool, lf_pool_t, suffix)


def _merge_body(o0, o1, o2, o3, g0, g1, g2, g3, w_ref, out_ref):
    acc = None
    for i, (o_ref, g_ref) in enumerate(((o0, g0), (o1, g1), (o2, g2), (o3, g3))):
        y = _sigmoid(g_ref[...]) * _dot(o_ref[...], w_ref[i])
        acc = y if acc is None else acc + y
    out_ref[...] = acc.astype(out_ref.dtype)


def _merge(branches, z, w_branch):
    m = z.shape[0]
    tm = _pick_tile(m, (512,))
    tn = 512
    assert D_MODEL % tn == 0 and OFF_GT % tn == 0
    o_spec = pl.BlockSpec((tm, BRANCH_W), lambda i, j: (i, 0))
    g_spec = lambda br: pl.BlockSpec((tm, tn), lambda i, j: (i, (OFF_GT + br * D_MODEL) // tn + j))
    return pl.pallas_call(
        _merge_body,
        grid=(m // tm, D_MODEL // tn),
        in_specs=[o_spec] * N_BRANCH + [g_spec(br) for br in range(N_BRANCH)]
        + [pl.BlockSpec((N_BRANCH, BRANCH_W, tn), lambda i, j: (0, 0, j))],
        out_specs=pl.BlockSpec((tm, tn), lambda i, j: (i, j)),
        out_shape=jax.ShapeDtypeStruct((m, D_MODEL), BF16),
        compiler_params=_cparams(("parallel", "arbitrary")),
        name="merge",
    )(*branches, z, z, z, z, w_branch)


def _pack_w_in(w):
    sizes = (FOX_W, FOX_W, FOX_W, FOX_HEADS, GLA_KW, GLA_KW, GLA_VW, GLA_VW, GLA_RANK,
             RET_W, RET_W, RET_W, RET_W, MEM_W, N_BRANCH * D_MODEL)
    cuts = np.concatenate([[0], np.cumsum(sizes)])
    seg = [w[:, int(cuts[i]):int(cuts[i + 1])] for i in range(len(sizes))]
    fq, fk, fv, ff, gq, gk, gv, gr, ga, rq, rk, rv, rg, mq, gt = seg
    pad = jnp.zeros((w.shape[0], LANES - FOX_HEADS - GLA_RANK), w.dtype)
    return jnp.concatenate([fq, fk, fv, gq, gk, gv, gr, rq, rk, rv, rg, mq, gt, ff, ga, pad], axis=1).astype(BF16)


def _pack_w_gla_a2(w2):
    out = jnp.zeros((LANES, GLA_KW), F32).at[SM_GA:SM_GA + GLA_RANK].set(w2)
    return out.astype(BF16)


def _rope_tables(pos):
    half = RET_DH // 2
    inv = ROPE_BASE ** (-jnp.arange(half, dtype=F32) / half)
    ang = pos.astype(F32)[:, None] * inv[None, :]
    return jnp.cos(ang), jnp.sin(ang)


def _retention_log_decay_table():
    lg = jnp.log1p(-jnp.exp2(-5.0 - jnp.arange(RET_HEADS, dtype=F32)))
    return jnp.broadcast_to(lg[:, None, None], (RET_HEADS, 1, LANES))


def _dense_tail(x, branches, z, lw):
    merged = _merge(branches, z, lw["w_branch"])
    x = _matmul(merged, lw["w_out"], res=x, name="out_proj")
    u = _matmul(x, lw["w_ff1"], g=lw["g_mlp"], act="relu2", out_dtype=BF16, name="ff1")
    return _matmul(u, lw["w_ff2"], res=x, name="ff2")


def kernel(x_prompt, x_sample, cache_fox_k, cache_fox_v, cache_fox_logf, state_gla, state_ret, cache_mem_k, cache_mem_v, page_table, mem_prompt, g_mix, w_in, b_fox_f, w_gla_a2, b_gla_a, g_gla, g_ret, w_mem_kv, w_branch, w_out, g_mlp, w_ff1, w_ff2, g_final):
    depth = w_in.shape[0]
    bp, t, d = x_prompt.shape
    bs, ts, _ = x_sample.shape
    assert ts == 1 and d == D_MODEL
    n_pool = cache_fox_k.shape[1]
    n_mem = mem_prompt.shape[1]
    past = page_table.shape[1] * PAGE_SIZE
    rows_s = -(-bs // BF16_ROWS) * BF16_ROWS

    cos_p, sin_p = _rope_tables(jnp.arange(t))
    cos_s, sin_s = _rope_tables(past + jnp.arange(ts))
    lg_tab = _retention_log_decay_table()
    k_pool = cache_fox_k.reshape(depth, n_pool, PAGE_SIZE, FOX_W)
    v_pool = cache_fox_v.reshape(depth, n_pool, PAGE_SIZE, FOX_W)
    lf_pool_t = jnp.swapaxes(cache_fox_logf, 2, 3)
    mem_k_s = cache_mem_k.reshape(depth, bs, n_mem, MEM_W)
    mem_v_s = cache_mem_v.reshape(depth, bs, n_mem, MEM_W)
    mem_prompt2 = mem_prompt.reshape(bp * n_mem, d)

    xp = x_prompt.reshape(bp * t, d)
    xs = jnp.zeros((rows_s, d), F32).at[:bs].set(x_sample.reshape(bs, d))
    outs = {k: [] for k in ("fk_p", "fv_p", "fl_p", "sg_p", "sr_p", "mk_p", "mv_p", "fk_s", "fv_s", "fl_s", "sg_s", "sr_s")}
    to_bf16_rows = lambda a: jnp.zeros((rows_s, a.shape[-1]), BF16).at[:bs].set(a.reshape(bs, -1).astype(BF16))

    for l in range(depth):
        lw = dict(
            w_branch=w_branch[l].astype(BF16), w_out=w_out[l].astype(BF16), g_mlp=g_mlp[l],
            w_ff1=w_ff1[l].astype(BF16), w_ff2=w_ff2[l].astype(BF16))
        w_in_p = _pack_w_in(w_in[l])
        w2p = _pack_w_gla_a2(w_gla_a2[l])

        z = _matmul(xp, w_in_p, g=g_mix[l], name="in_proj")
        z3 = z.reshape(bp, t, N_PACK)
        ff_t = jnp.swapaxes(z3[:, :, OFF_SM + SM_FF:OFF_SM + SM_FF + FOX_HEADS], 1, 2)
        lf_t, c_t = _fox_gate(ff_t, b_fox_f[l])
        o_fox = _fox_attn(z3, jnp.swapaxes(c_t, 1, 2), c_t)
        mkv = _matmul(mem_prompt2, w_mem_kv[l].astype(BF16), name="mem_kv")
        mkv4 = mkv.reshape(1, bp, n_mem, 2 * MEM_W)
        o_mem = _mem_attn(z3, mkv4, mkv4, layer=0, kcol=0, vcol=1, out_dtype=BF16)
        o_gla, s_gla = _gla_prompt(z3, w2p, b_gla_a[l], g_gla[l])
        o_ret, s_ret = _ret_prompt(z3, cos_p, sin_p, lg_tab, g_ret[l])
        flat = lambda a: a.reshape(bp * t, -1)
        xp = _dense_tail(xp, [flat(o_fox), flat(o_gla), flat(o_ret), flat(o_mem)], z, lw)
        outs["fk_p"].append(z3[:, :, OFF_FK:OFF_FK + FOX_W].reshape(bp, t, FOX_HEADS, FOX_DH))
        outs["fv_p"].append(z3[:, :, OFF_FV:OFF_FV + FOX_W].reshape(bp, t, FOX_HEADS, FOX_DH))
        outs["fl_p"].append(jnp.swapaxes(lf_t, 1, 2))
        outs["sg_p"].append(s_gla)
        outs["sr_p"].append(s_ret)
        outs["mk_p"].append(mkv4[0, :, :, :MEM_W].reshape(bp, n_mem, MEM_HEADS, MEM_DH))
        outs["mv_p"].append(mkv4[0, :, :, MEM_W:].reshape(bp, n_mem, MEM_HEADS, MEM_DH))

        zs = _matmul(xs, w_in_p, g=g_mix[l], name="in_proj_s")
        zs3 = zs[:bs].reshape(bs, 1, N_PACK)
        ff_col = zs[:bs, OFF_SM + SM_FF:OFF_SM + SM_FF + FOX_HEADS].reshape(bs, FOX_HEADS, 1)
        o_fox_s, lf_s = _fox_decode(page_table, zs3, ff_col, b_fox_f[l].reshape(FOX_HEADS, 1),
                                    k_pool, v_pool, lf_pool_t, l)
        o_mem_s = _mem_attn(zs3, mem_k_s, mem_v_s, layer=l, kcol=0, vcol=0, out_dtype=F32)
        o_gla_s, sg_s = _gla_step(zs3, w2p, b_gla_a[l], g_gla[l], state_gla, l)
        o_ret_s, sr_s = _ret_step(zs3, cos_s, sin_s, lg_tab, g_ret[l], state_ret, l)
        xs = _dense_tail(xs, [to_bf16_rows(a) for a in (o_fox_s, o_gla_s, o_ret_s, o_mem_s)], zs, lw)
        outs["fk_s"].append(zs3[:, :, OFF_FK:OFF_FK + FOX_W].reshape(bs, 1, FOX_HEADS, FOX_DH))
        outs["fv_s"].append(zs3[:, :, OFF_FV:OFF_FV + FOX_W].reshape(bs, 1, FOX_HEADS, FOX_DH))
        outs["fl_s"].append(lf_s.reshape(bs, 1, FOX_HEADS))
        outs["sg_s"].append(sg_s)
        outs["sr_s"].append(sr_s)

    y_prompt = _rmsnorm(xp, g_final).reshape(bp, t, d)
    y_sample = _rmsnorm(xs, g_final)[:bs].reshape(bs, 1, d)
    st = lambda k: jnp.stack(outs[k])
    return (y_prompt, y_sample, st("fk_p"), st("fv_p"), st("fl_p"), st("sg_p"), st("sr_p"), st("mk_p"), st("mv_p"),
            st("fk_s"), st("fv_s"), st("fl_s"), st("sg_s"), st("sr_s"))
```

```python
import functools

import numpy as np
import jax
import jax.numpy as jnp
from jax import lax
from jax.experimental import pallas as pl
from jax.experimental.pallas import tpu as pltpu

F32 = jnp.float32
BF16 = jnp.bfloat16

D_MODEL = 2048
FOX_HEADS = 8
FOX_DH = D_MODEL // 16
FOX_W = FOX_HEADS * FOX_DH
GLA_HEADS = 4
GLA_DK = D_MODEL // 16
GLA_DV = D_MODEL // 8
GLA_KW = GLA_HEADS * GLA_DK
GLA_VW = GLA_HEADS * GLA_DV
GLA_RANK = 16
GLA_TAU = 16.0
RET_HEADS = 4
RET_DH = D_MODEL // 8
RET_W = RET_HEADS * RET_DH
MEM_HEADS = 4
MEM_DH = D_MODEL // 8
MEM_W = MEM_HEADS * MEM_DH
N_BRANCH = 4
BRANCH_W = D_MODEL // 2
D_FF = 4 * D_MODEL
PAGE_SIZE = 128
ROPE_BASE = 10000.0
EPS = 1e-6
LOG2E = 1.4426950408889634

LANES = 128
SUBLANES = 8
BF16_ROWS = 16
VMEM_LIMIT_BYTES = 56 * 1024 * 1024

OFF_FQ = 0
OFF_FK = OFF_FQ + FOX_W
OFF_FV = OFF_FK + FOX_W
OFF_GQ = OFF_FV + FOX_W
OFF_GK = OFF_GQ + GLA_KW
OFF_GV = OFF_GK + GLA_KW
OFF_GR = OFF_GV + GLA_VW
OFF_RQ = OFF_GR + GLA_VW
OFF_RK = OFF_RQ + RET_W
OFF_RV = OFF_RK + RET_W
OFF_RG = OFF_RV + RET_W
OFF_MQ = OFF_RG + RET_W
OFF_GT = OFF_MQ + MEM_W
OFF_SM = OFF_GT + N_BRANCH * D_MODEL
SM_FF = 0
SM_GA = FOX_HEADS
N_PACK = OFF_SM + LANES

GLA_CHUNK = 64
GLA_SUB = 8
GLA_SUB_LOG2 = 3
GLA_CHUNKS_PER_STEP = 4
FOX_PAGES_PER_STEP = 4
FOX_DH_LOG2 = 7
assert GLA_SUB == 1 << GLA_SUB_LOG2 and FOX_DH == 1 << FOX_DH_LOG2 and LANES & (LANES - 1) == 0
RET_CHUNK = 256


def _cparams(sem):
    return pltpu.CompilerParams(dimension_semantics=sem, vmem_limit_bytes=VMEM_LIMIT_BYTES)


def _log_sigmoid(x):
    return jnp.minimum(x, 0.0) - jnp.log1p(jnp.exp(-jnp.abs(x)))


def _sigmoid(x):
    return 1.0 / (1.0 + jnp.exp(-x))


def _dot(a, b):
    return jnp.dot(a, b, preferred_element_type=F32)


def _dot_nt(a, b):
    return lax.dot_general(a, b, (((1,), (1,)), ((), ())), preferred_element_type=F32)


def _dot_tn(a, b):
    return lax.dot_general(a, b, (((0,), (0,)), ((), ())), preferred_element_type=F32)


def _dot01(m01, x):
    hi = x.astype(BF16)
    r1 = x - hi.astype(F32)
    mid = r1.astype(BF16)
    lo = (r1 - mid.astype(F32)).astype(BF16)
    return _dot(m01, hi) + _dot(m01, mid) + _dot(m01, lo)


def _row_to_cols(row):
    return jnp.broadcast_to(row, (LANES, LANES)).T


def _head_norm_gate(o, gain, gate):
    mu = jnp.mean(o, axis=1, keepdims=True)
    d = o - mu
    var = jnp.mean(d * d, axis=1, keepdims=True)
    y = d * lax.rsqrt(var + EPS) * gain
    return y * (gate * _sigmoid(gate))


def _mm_body(*refs, norm, act, has_res, stage, bf16_copy):
    it = iter(refs)
    x_ref = next(it)
    g_ref = next(it) if norm else None
    w_ref = next(it)
    r_ref = next(it) if has_res else None
    o_ref = next(it)
    ob_ref = next(it) if bf16_copy else None
    h_ref = next(it) if stage else None
    if stage:
        @pl.when(pl.program_id(1) == 0)
        def _():
            x = x_ref[...].astype(F32)
            if norm:
                ms = jnp.mean(x * x, axis=-1, keepdims=True)
                x = x * lax.rsqrt(ms + EPS) * g_ref[...]
            h_ref[...] = x.astype(BF16)

        h = h_ref[...]
    else:
        h = x_ref[...]
    acc = _dot(h, w_ref[...])
    if act == "relu2":
        acc = jnp.square(jnp.maximum(acc, 0.0))
    if has_res:
        acc = acc + r_ref[...]
    o_ref[...] = acc.astype(o_ref.dtype)
    if bf16_copy:
        ob_ref[...] = acc.astype(BF16)


def _pick_tile(n, candidates):
    for c in candidates:
        if n % c == 0:
            return c
    return n


def _matmul(x, w, *, g=None, res=None, act=None, out_dtype=F32, bf16_copy=False, name="proj"):
    m, k = x.shape
    n = w.shape[1]
    norm = g is not None
    stage = norm or x.dtype != BF16
    if m >= 1024:
        tm = 1024 if k <= 2048 else 512
    else:
        tm = m
    tn = _pick_tile(n, (1152, 1024, 512))
    if k > 2048:
        tn = min(tn, 512)
    assert m % tm == 0 and n % tn == 0 and tm % BF16_ROWS == 0
    in_specs = [pl.BlockSpec((tm, k), lambda i, j: (i, 0))]
    args = [x]
    if norm:
        in_specs.append(pl.BlockSpec((1, k), lambda i, j: (0, 0)))
        args.append(g.reshape(1, k).astype(F32))
    in_specs.append(pl.BlockSpec((k, tn), lambda i, j: (0, j)))
    args.append(w)
    if res is not None:
        in_specs.append(pl.BlockSpec((tm, tn), lambda i, j: (i, j)))
        args.append(res)
    scratch = [pltpu.VMEM((tm, k), BF16)] if stage else []
    o_spec = pl.BlockSpec((tm, tn), lambda i, j: (i, j))
    o_shape = jax.ShapeDtypeStruct((m, n), out_dtype)
    return pl.pallas_call(
        functools.partial(_mm_body, norm=norm, act=act, has_res=res is not None, stage=stage, bf16_copy=bf16_copy),
        grid=(m // tm, n // tn),
        in_specs=in_specs,
        out_specs=[o_spec, o_spec] if bf16_copy else o_spec,
        out_shape=[o_shape, jax.ShapeDtypeStruct((m, n), BF16)] if bf16_copy else o_shape,
        scratch_shapes=scratch,
        compiler_params=_cparams(("parallel", "arbitrary")),
        name=name,
    )(*args)


def _rmsnorm_body(x_ref, g_ref, o_ref):
    x = x_ref[...]
    ms = jnp.mean(x * x, axis=-1, keepdims=True)
    o_ref[...] = x * lax.rsqrt(ms + EPS) * g_ref[...]


def _rmsnorm(x, g):
    m, k = x.shape
    tm = _pick_tile(m, (512,))
    return pl.pallas_call(
        _rmsnorm_body,
        grid=(m // tm,),
        in_specs=[pl.BlockSpec((tm, k), lambda i: (i, 0)), pl.BlockSpec((1, k), lambda i: (0, 0))],
        out_specs=pl.BlockSpec((tm, k), lambda i: (i, 0)),
        out_shape=jax.ShapeDtypeStruct((m, k), F32),
        compiler_params=_cparams(("parallel",)),
        name="final_norm",
    )(x, g.reshape(1, k))


def _fox_gate_body(x_ref, b_ref, lf_ref, c_ref):
    lf = _log_sigmoid(x_ref[...] + b_ref[...])
    lf_ref[...] = lf
    t = lf.shape[1]
    lane = lax.broadcasted_iota(jnp.int32, lf.shape, 1)
    c = lf
    shift = 1
    while shift < t:
        c = c + jnp.where(lane >= shift, pltpu.roll(c, shift, axis=1), 0.0)
        shift *= 2
    c_ref[...] = c


def _fox_gate(ff_t, bias):
    b, h, t = ff_t.shape
    spec = pl.BlockSpec((None, h, t), lambda i: (i, 0, 0))
    return pl.pallas_call(
        _fox_gate_body,
        grid=(b,),
        in_specs=[spec, pl.BlockSpec((h, 1), lambda i: (0, 0))],
        out_specs=[spec, spec],
        out_shape=[jax.ShapeDtypeStruct((b, h, t), F32)] * 2,
        compiler_params=_cparams(("parallel",)),
        name="fox_gate",
    )(ff_t, bias.reshape(h, 1))


def _fox_attn_body(q_ref, k_ref, v_ref, ck_ref, o_ref, q_sc, m_sc, l_sc, acc_sc, *, tile):
    qi = pl.program_id(1)
    ki = pl.program_id(2)
    nk = pl.num_programs(2)

    @pl.when(ki == 0)
    def _():
        q_sc[...] = (q_ref[...] * (FOX_DH ** -0.5 * LOG2E)).astype(BF16)
        m_sc[...] = jnp.full(m_sc.shape, -jnp.inf, F32)
        l_sc[...] = jnp.zeros(l_sc.shape, F32)
        acc_sc[...] = jnp.zeros(acc_sc.shape, F32)

    def step(on_diagonal):
        ck = ck_ref[...] * LOG2E
        if on_diagonal:
            row = lax.broadcasted_iota(jnp.int32, (tile, tile), 0)
            col = lax.broadcasted_iota(jnp.int32, (tile, tile), 1)
            visible = col <= row
        for h in range(FOX_HEADS):
            sl = slice(h * FOX_DH, (h + 1) * FOX_DH)
            s = _dot_nt(q_sc[:, sl], k_ref[:, sl]) - ck[h:h + 1, :]
            if on_diagonal:
                s = jnp.where(visible, s, -jnp.inf)
            m_prev = m_sc[h]
            m_new = jnp.maximum(m_prev, jnp.max(s, axis=1, keepdims=True))
            alpha = jnp.exp2(m_prev - m_new)
            p = jnp.exp2(s - m_new[:, :1])
            l_sc[h] = alpha * l_sc[h] + jnp.sum(p, axis=1, keepdims=True)
            acc_sc[:, sl] = alpha * acc_sc[:, sl] + _dot(p.astype(BF16), v_ref[:, sl])
            m_sc[h] = m_new

    pl.when(ki < qi)(functools.partial(step, False))
    pl.when(ki == qi)(functools.partial(step, True))

    @pl.when(ki == nk - 1)
    def _():
        for h in range(FOX_HEADS):
            sl = slice(h * FOX_DH, (h + 1) * FOX_DH)
            o_ref[:, sl] = (acc_sc[:, sl] / l_sc[h]).astype(o_ref.dtype)


def _fox_attn(z3, zb3, c_row):
    b, t, _ = z3.shape
    tile = _pick_tile(t, (512, 256, 128))
    n = t // tile
    qcol, kcol, vcol = OFF_FQ // FOX_W, OFF_FK // FOX_W, OFF_FV // FOX_W
    return pl.pallas_call(
        functools.partial(_fox_attn_body, tile=tile),
        grid=(b, n, n),
        in_specs=[
            pl.BlockSpec((None, tile, FOX_W), lambda bi, qi, ki: (bi, qi, qcol)),
            pl.BlockSpec((None, tile, FOX_W), lambda bi, qi, ki: (bi, jnp.minimum(ki, qi), kcol)),
            pl.BlockSpec((None, tile, FOX_W), lambda bi, qi, ki: (bi, jnp.minimum(ki, qi), vcol)),
            pl.BlockSpec((None, FOX_HEADS, tile), lambda bi, qi, ki: (bi, 0, jnp.minimum(ki, qi))),
        ],
        out_specs=pl.BlockSpec((None, tile, FOX_W), lambda bi, qi, ki: (bi, qi, 0)),
        out_shape=jax.ShapeDtypeStruct((b, t, FOX_W), BF16),
        scratch_shapes=[
            pltpu.VMEM((tile, FOX_W), BF16),
            pltpu.VMEM((FOX_HEADS, tile, FOX_DH), F32),
            pltpu.VMEM((FOX_HEADS, tile, FOX_DH), F32),
            pltpu.VMEM((tile, FOX_W), F32),
        ],
        compiler_params=_cparams(("parallel", "parallel", "arbitrary")),
        name="fox_attn",
    )(z3, zb3, zb3, c_row)


def _mem_attn_body(q_ref, k_ref, v_ref, o_ref, *, rows):
    scale = MEM_DH ** -0.5
    for h in range(MEM_HEADS):
        sl = slice(h * MEM_DH, (h + 1) * MEM_DH)
        q = q_ref[:, sl]
        if rows < SUBLANES:
            q = jnp.broadcast_to(q, (SUBLANES, MEM_DH))
        s = _dot_nt(q.astype(BF16), k_ref[:, sl].astype(BF16)) * scale
        m = jnp.max(s, axis=1, keepdims=True)
        p = jnp.exp(s - m)
        p = p / jnp.sum(p, axis=1, keepdims=True)
        o = _dot(p.astype(BF16), v_ref[:, sl].astype(BF16))
        o_ref[:, sl] = o[:rows].astype(o_ref.dtype)


def _mem_attn(z3, k4, v4, *, layer, kcol, vcol, out_dtype):
    b, t, _ = z3.shape
    n_mem = k4.shape[2]
    tq = _pick_tile(t, (512, 256, 128))
    qcol = OFF_MQ // MEM_W
    return pl.pallas_call(
        functools.partial(_mem_attn_body, rows=tq),
        grid=(b, t // tq),
        in_specs=[
            pl.BlockSpec((None, tq, MEM_W), lambda bi, qi: (bi, qi, qcol)),
            pl.BlockSpec((None, None, n_mem, MEM_W), lambda bi, qi: (layer, bi, 0, kcol)),
            pl.BlockSpec((None, None, n_mem, MEM_W), lambda bi, qi: (layer, bi, 0, vcol)),
        ],
        out_specs=pl.BlockSpec((None, tq, MEM_W), lambda bi, qi: (bi, qi, 0)),
        out_shape=jax.ShapeDtypeStruct((b, t, MEM_W), out_dtype),
        compiler_params=_cparams(("parallel", "parallel")),
        name="mem_attn",
    )(z3, k4, v4)


def _rotary(x, cos, sin):
    half = x.shape[1] // 2
    x1, x2 = x[:, :half], x[:, half:]
    return jnp.concatenate([x1 * cos - x2 * sin, x1 * sin + x2 * cos], axis=1)


def _ret_body(q_ref, k_ref, v_ref, g_ref, cos_ref, sin_ref, lg_ref, gn_ref, o_ref, s_ref, st_sc, *, chunk):
    c = pl.program_id(2)
    nc = pl.num_programs(2)

    @pl.when(c == 0)
    def _():
        st_sc[...] = jnp.zeros(st_sc.shape, F32)

    lg = lg_ref[:, :1]
    cos, sin = cos_ref[...], sin_ref[...]
    q = _rotary(q_ref[...], cos, sin)
    k = _rotary(k_ref[...], cos, sin) * (RET_DH ** -0.5)
    v = v_ref[...].astype(BF16)
    row = lax.broadcasted_iota(jnp.int32, (chunk, chunk), 0)
    col = lax.broadcasted_iota(jnp.int32, (chunk, chunk), 1)
    decay = jnp.where(col <= row, jnp.exp((row - col).astype(F32) * lg), 0.0)
    t = lax.broadcasted_iota(jnp.int32, (chunk, 1), 0).astype(F32)
    a = _dot_nt(q.astype(BF16), k.astype(BF16)) * decay
    s0 = st_sc[...]
    o = _dot(a.astype(BF16), v) + jnp.exp((t + 1.0) * lg) * _dot(q.astype(BF16), s0.astype(BF16))
    k_end = k * jnp.exp((chunk - 1.0 - t) * lg)
    s_new = jnp.exp(chunk * lg) * s0 + _dot_tn(k_end.astype(BF16), v)
    st_sc[...] = s_new
    o_ref[...] = _head_norm_gate(o, gn_ref[...], g_ref[...]).astype(o_ref.dtype)

    @pl.when(c == nc - 1)
    def _():
        s_ref[...] = s_new


def _ret_prompt(z3, cos, sin, lg_tab, g_ret):
    b, t, _ = z3.shape
    chunk = _pick_tile(t, (RET_CHUNK, 128, 64))
    cb = lambda off: off // RET_DH
    qcol, kcol, vcol, gcol = cb(OFF_RQ), cb(OFF_RK), cb(OFF_RV), cb(OFF_RG)
    zspec = lambda col: pl.BlockSpec((None, chunk, RET_DH), lambda bi, h, c: (bi, c, col + h))
    return pl.pallas_call(
        functools.partial(_ret_body, chunk=chunk),
        grid=(b, RET_HEADS, t // chunk),
        in_specs=[
            zspec(qcol), zspec(kcol), zspec(vcol), zspec(gcol),
            pl.BlockSpec((chunk, RET_DH // 2), lambda bi, h, c: (c, 0)),
            pl.BlockSpec((chunk, RET_DH // 2), lambda bi, h, c: (c, 0)),
            pl.BlockSpec((None, 1, LANES), lambda bi, h, c: (h, 0, 0)),
            pl.BlockSpec((1, RET_DH), lambda bi, h, c: (0, h)),
        ],
        out_specs=[
            pl.BlockSpec((None, chunk, RET_DH), lambda bi, h, c: (bi, c, h)),
            pl.BlockSpec((None, None, RET_DH, RET_DH), lambda bi, h, c: (bi, h, 0, 0)),
        ],
        out_shape=[
            jax.ShapeDtypeStruct((b, t, RET_W), BF16),
            jax.ShapeDtypeStruct((b, RET_HEADS, RET_DH, RET_DH), F32),
        ],
        scratch_shapes=[pltpu.VMEM((RET_DH, RET_DH), F32)],
        compiler_params=_cparams(("parallel", "parallel", "arbitrary")),
        name="ret_prompt",
    )(z3, z3, z3, z3, cos, sin, lg_tab, g_ret.reshape(1, RET_W))


def _gla_loga(zs, w2_ref, ba_ref):
    pre = _dot(zs.astype(BF16), w2_ref[...]) + ba_ref[...]
    return _log_sigmoid(pre) * (1.0 / GLA_TAU)


def _gla_chunk(q, k, v_ref, gr_ref, gn_ref, o_ref, loga, st, tri, hsum, rows):
    chunk = q.shape[0]
    n_sub = chunk // GLA_SUB
    b = _dot01(tri, loga)
    row = lax.broadcasted_iota(jnp.int32, (chunk, GLA_KW), 0)
    b_last = b[chunk - 1:chunk, :]
    beta = jnp.zeros_like(b)
    for i in range(1, n_sub):
        beta = jnp.where(row >= i * GLA_SUB, b[i * GLA_SUB - 1:i * GLA_SUB, :], beta)
    q_rel = q * jnp.exp(b - beta)
    q_abs = q * jnp.exp(b)
    k_end = k * jnp.exp(b_last - b)
    k_rel = []
    for i in range(1, n_sub):
        bi = b[i * GLA_SUB - 1:i * GLA_SUB, :]
        k_rel.append(jnp.where(row < i * GLA_SUB, k * jnp.exp(jnp.minimum(bi - b, 0.0)), 0.0).astype(BF16))

    prods = []
    for delta in range(GLA_SUB):
        kd = k if delta == 0 else pltpu.roll(k, delta, axis=0)
        bd = b if delta == 0 else pltpu.roll(b, delta, axis=0)
        prods.append((q * kd * jnp.exp(jnp.minimum(b - bd, 0.0))).astype(BF16))
    sums = _dot(jnp.concatenate(prods, axis=0), hsum)
    rowb = lax.broadcasted_iota(jnp.int32, (chunk, GLA_HEADS * LANES), 0)
    colb = lax.broadcasted_iota(jnp.int32, (chunk, GLA_HEADS * LANES), 1) & (LANES - 1)
    rsub = rowb & (GLA_SUB - 1)
    band = jnp.zeros((chunk, GLA_HEADS * LANES), F32)
    for delta in range(GLA_SUB):
        keep = jnp.logical_and(colb == rowb - delta, rsub >= delta)
        band = band + jnp.where(keep, sums[delta * chunk:(delta + 1) * chunk], 0.0)

    sub_of_row = row[:, :GLA_DK] >> GLA_SUB_LOG2
    new_st = []
    for h in range(GLA_HEADS):
        ks = slice(h * GLA_DK, (h + 1) * GLA_DK)
        vs = slice(h * GLA_DV, (h + 1) * GLA_DV)
        v = v_ref[rows, vs].astype(BF16)
        lhs = jnp.concatenate(
            [jnp.where(sub_of_row == i, q_rel[:, ks], 0.0) for i in range(1, n_sub)], axis=1).astype(BF16)
        rhs = jnp.concatenate([kr[:, ks] for kr in k_rel], axis=1)
        a = _dot_nt(lhs, rhs) + band[:, h * LANES:h * LANES + chunk]
        o = _dot(a.astype(BF16), v) + _dot_nt(q_abs[:, ks].astype(BF16), st[h].astype(BF16))
        new_st.append(st[h] * jnp.exp(b_last[:, ks]) + _dot_tn(v, k_end[:, ks].astype(BF16)))
        o_ref[rows, vs] = _head_norm_gate(o, gn_ref[:, vs], gr_ref[rows, vs]).astype(o_ref.dtype)
    return new_st


def _gla_body(q_ref, k_ref, v_ref, gr_ref, zs_ref, w2_ref, ba_ref, gn_ref, tri_ref, hsum_ref,
              o_ref, s_ref, st_sc, *, chunk, n_chunks):
    c = pl.program_id(1)
    nc = pl.num_programs(1)

    @pl.when(c == 0)
    def _():
        st_sc[...] = jnp.zeros(st_sc.shape, F32)

    loga = _gla_loga(zs_ref[...], w2_ref, ba_ref)
    tri, hsum = tri_ref[...], hsum_ref[...]
    st = [st_sc[h] for h in range(GLA_HEADS)]
    for ci in range(n_chunks):
        rows = slice(ci * chunk, (ci + 1) * chunk)
        q = q_ref[rows, :] * (GLA_DK ** -0.5)
        st = _gla_chunk(q, k_ref[rows, :], v_ref, gr_ref, gn_ref, o_ref, loga[rows], st, tri, hsum, rows)
    for h in range(GLA_HEADS):
        st_sc[h] = st[h]

    @pl.when(c == nc - 1)
    def _():
        for h in range(GLA_HEADS):
            s_ref[h] = st[h].T


def _gla_consts(chunk):
    r = np.arange(chunk)
    tri = (r[None, :] <= r[:, None]).astype(np.float32)
    lane = np.arange(GLA_KW)
    out = np.arange(GLA_HEADS * LANES)
    hsum = (lane[:, None] // GLA_DK == out[None, :] // LANES).astype(np.float32)
    return jnp.asarray(tri, BF16), jnp.asarray(hsum, BF16)


def _gla_prompt(z3, w2p, b_a, g_gla):
    b, t, _ = z3.shape
    chunk = GLA_CHUNK
    n_chunks = _pick_tile(t // chunk, (GLA_CHUNKS_PER_STEP, 2, 1))
    rows = chunk * n_chunks
    assert t % rows == 0
    tri, hsum = _gla_consts(chunk)
    full = lambda shape: pl.BlockSpec(shape, lambda bi, c: (0,) * len(shape))
    return pl.pallas_call(
        functools.partial(_gla_body, chunk=chunk, n_chunks=n_chunks),
        grid=(b, t // rows),
        in_specs=[
            pl.BlockSpec((None, rows, GLA_KW), lambda bi, c: (bi, c, OFF_GQ // GLA_KW)),
            pl.BlockSpec((None, rows, GLA_KW), lambda bi, c: (bi, c, OFF_GK // GLA_KW)),
            pl.BlockSpec((None, rows, GLA_VW), lambda bi, c: (bi, c, OFF_GV // GLA_VW)),
            pl.BlockSpec((None, rows, GLA_VW), lambda bi, c: (bi, c, OFF_GR // GLA_VW)),
            pl.BlockSpec((None, rows, LANES), lambda bi, c: (bi, c, OFF_SM // LANES)),
            full((LANES, GLA_KW)), full((1, GLA_KW)), full((1, GLA_VW)),
            full((chunk, chunk)), full((GLA_KW, GLA_HEADS * LANES)),
        ],
        out_specs=[
            pl.BlockSpec((None, rows, GLA_VW), lambda bi, c: (bi, c, 0)),
            pl.BlockSpec((None, GLA_HEADS, GLA_DK, GLA_DV), lambda bi, c: (bi, 0, 0, 0)),
        ],
        out_shape=[
            jax.ShapeDtypeStruct((b, t, GLA_VW), BF16),
            jax.ShapeDtypeStruct((b, GLA_HEADS, GLA_DK, GLA_DV), F32),
        ],
        scratch_shapes=[pltpu.VMEM((GLA_HEADS, GLA_DV, GLA_DK), F32)],
        compiler_params=_cparams(("parallel", "arbitrary")),
        name="gla_prompt",
    )(z3, z3, z3, z3, z3, w2p, b_a.reshape(1, GLA_KW), g_gla.reshape(1, GLA_VW), tri, hsum)


def _gla_step_body(q_ref, k_ref, v_ref, gr_ref, zs_ref, w2_ref, ba_ref, gn_ref, s0_ref, o_ref, s_ref):
    zs = jnp.broadcast_to(zs_ref[...], (BF16_ROWS, LANES))
    loga = _gla_loga(zs, w2_ref, ba_ref)[:1]
    q = q_ref[...] * (GLA_DK ** -0.5)
    k = k_ref[...]
    for h in range(GLA_HEADS):
        ks = slice(h * GLA_DK, (h + 1) * GLA_DK)
        vs = slice(h * GLA_DV, (h + 1) * GLA_DV)
        widen = lambda m: jnp.concatenate([m] * (GLA_DV // LANES), axis=1)
        a_c = widen(_row_to_cols(jnp.exp(loga[:, ks])))
        k_c = widen(_row_to_cols(k[:, ks]))
        q_c = widen(_row_to_cols(q[:, ks]))
        s_new = a_c * s0_ref[h] + k_c * v_ref[:, vs]
        s_ref[h] = s_new
        o = jnp.sum(q_c * s_new, axis=0, keepdims=True)
        o_ref[:, vs] = _head_norm_gate(o, gn_ref[:, vs], gr_ref[:, vs])


def _gla_step(zs3, w2p, b_a, g_gla, state, layer):
    b = zs3.shape[0]
    full = lambda shape: pl.BlockSpec(shape, lambda bi: (0,) * len(shape))
    return pl.pallas_call(
        _gla_step_body,
        grid=(b,),
        in_specs=[
            pl.BlockSpec((None, 1, GLA_KW), lambda bi: (bi, 0, OFF_GQ // GLA_KW)),
            pl.BlockSpec((None, 1, GLA_KW), lambda bi: (bi, 0, OFF_GK // GLA_KW)),
            pl.BlockSpec((None, 1, GLA_VW), lambda bi: (bi, 0, OFF_GV // GLA_VW)),
            pl.BlockSpec((None, 1, GLA_VW), lambda bi: (bi, 0, OFF_GR // GLA_VW)),
            pl.BlockSpec((None, 1, LANES), lambda bi: (bi, 0, OFF_SM // LANES)),
            full((LANES, GLA_KW)), full((1, GLA_KW)), full((1, GLA_VW)),
            pl.BlockSpec((None, None, GLA_HEADS, GLA_DK, GLA_DV), lambda bi: (layer, bi, 0, 0, 0)),
        ],
        out_specs=[
            pl.BlockSpec((None, 1, GLA_VW), lambda bi: (bi, 0, 0)),
            pl.BlockSpec((None, GLA_HEADS, GLA_DK, GLA_DV), lambda bi: (bi, 0, 0, 0)),
        ],
        out_shape=[
            jax.ShapeDtypeStruct((b, 1, GLA_VW), F32),
            jax.ShapeDtypeStruct((b, GLA_HEADS, GLA_DK, GLA_DV), F32),
        ],
        compiler_params=_cparams(("parallel",)),
        name="gla_step",
    )(zs3, zs3, zs3, zs3, zs3, w2p, b_a.reshape(1, GLA_KW), g_gla.reshape(1, GLA_VW), state)


def _ret_step_body(q_ref, k_ref, v_ref, g_ref, cos_ref, sin_ref, lg_ref, gn_ref, s0_ref, o_ref, s_ref):
    cos, sin = cos_ref[...], sin_ref[...]
    half = RET_DH // 2
    for h in range(RET_HEADS):
        sl = slice(h * RET_DH, (h + 1) * RET_DH)
        q = _rotary(q_ref[:, sl], cos, sin)
        k = _rotary(k_ref[:, sl], cos, sin) * (RET_DH ** -0.5)
        cols = lambda r: jnp.concatenate(
            [jnp.concatenate([_row_to_cols(r[:, i * half:(i + 1) * half])] * 2, axis=1) for i in range(2)], axis=0)
        gamma = jnp.exp(lg_ref[h][:, :1])
        s_new = gamma * s0_ref[h] + cols(k) * v_ref[:, sl]
        s_ref[h] = s_new
        o = jnp.sum(cols(q) * s_new, axis=0, keepdims=True)
        o_ref[:, sl] = _head_norm_gate(o, gn_ref[:, sl], g_ref[:, sl])


def _ret_step(zs3, cos, sin, lg_tab, g_ret, state, layer):
    b = zs3.shape[0]
    full = lambda shape: pl.BlockSpec(shape, lambda bi: (0,) * len(shape))
    zspec = lambda off: pl.BlockSpec((None, 1, RET_W), lambda bi: (bi, 0, off // RET_W))
    return pl.pallas_call(
        _ret_step_body,
        grid=(b,),
        in_specs=[
            zspec(OFF_RQ), zspec(OFF_RK), zspec(OFF_RV), zspec(OFF_RG),
            full((1, RET_DH // 2)), full((1, RET_DH // 2)), full((RET_HEADS, 1, LANES)), full((1, RET_W)),
            pl.BlockSpec((None, None, RET_HEADS, RET_DH, RET_DH), lambda bi: (layer, bi, 0, 0, 0)),
        ],
        out_specs=[
            pl.BlockSpec((None, 1, RET_W), lambda bi: (bi, 0, 0)),
            pl.BlockSpec((None, RET_HEADS, RET_DH, RET_DH), lambda bi: (bi, 0, 0, 0)),
        ],
        out_shape=[
            jax.ShapeDtypeStruct((b, 1, RET_W), F32),
            jax.ShapeDtypeStruct((b, RET_HEADS, RET_DH, RET_DH), F32),
        ],
        compiler_params=_cparams(("parallel",)),
        name="ret_step",
    )(zs3, zs3, zs3, zs3, cos, sin, lg_tab, g_ret.reshape(1, RET_W), state)


def _page_suffix_body(x_ref, o_ref):
    x = x_ref[...]
    width = x.shape[1]
    lane = lax.broadcasted_iota(jnp.int32, x.shape, 1)
    incl = x
    shift = FOX_HEADS
    while shift < width:
        moved = pltpu.roll(incl, width - shift, axis=1)
        incl = incl + jnp.where(lane < width - shift, moved, 0.0)
        shift *= 2
    o_ref[...] = incl


def _page_suffix(lf_pages):
    n, width = lf_pages.shape
    rows = _pick_tile(n, (64, 32, 16, 8))
    spec = pl.BlockSpec((rows, width), lambda i: (i, 0))
    return pl.pallas_call(
        _page_suffix_body,
        grid=(n // rows,),
        in_specs=[spec],
        out_specs=spec,
        out_shape=jax.ShapeDtypeStruct((n, width), F32),
        compiler_params=_cparams(("parallel",)),
        name="page_suffix",
    )(lf_pages)


def _fox_decode_body(*refs, pages):
    q_ref, kn_ref, vn_ref, ff_ref, bf_ref = refs[1:6]
    kp_refs = refs[6:6 + pages]
    vp_refs = refs[6 + pages:6 + 2 * pages]
    lp_refs = refs[6 + 2 * pages:6 + 3 * pages]
    sf_refs = refs[6 + 3 * pages:6 + 4 * pages]
    o_ref, lf_ref, m_sc, l_sc, tail_sc, acc_sc = refs[6 + 4 * pages:]
    g = pl.program_id(1)
    n_groups = pl.num_programs(1)
    scale = FOX_DH ** -0.5
    width = PAGE_SIZE * FOX_HEADS
    sub = lax.broadcasted_iota(jnp.int32, (FOX_HEADS, width), 0)
    lane = lax.broadcasted_iota(jnp.int32, (FOX_HEADS, width), 1)
    own = (lane & (FOX_HEADS - 1)) == sub
    q = q_ref[...]
    qb = q.astype(BF16)

    @pl.when(g == 0)
    def _():
        lf_new = _log_sigmoid(ff_ref[...] + bf_ref[...])
        lf_ref[...] = lf_new
        tail_sc[...] = jnp.broadcast_to(lf_new, tail_sc.shape)
        s_new = jnp.sum(q * kn_ref[...], axis=1, keepdims=True) * scale
        m_sc[...] = jnp.broadcast_to(s_new, m_sc.shape)
        l_sc[...] = jnp.ones(l_sc.shape, F32)
        acc_sc[...] = vn_ref[...]

    tail = tail_sc[:, :1]
    scores = []
    for i in range(pages):
        incl = jnp.broadcast_to(sf_refs[i][...], (FOX_HEADS, width))
        bias = tail + (incl - lp_refs[i][...])
        s = _dot_nt(qb, kp_refs[i][...].astype(BF16)) * scale + bias
        scores.append(jnp.where(own, s, -jnp.inf))
        tail = tail + jnp.sum(jnp.where(lane == sub, incl, 0.0), axis=1, keepdims=True)
    m_prev = m_sc[:, :1]
    m_new = m_prev
    for s in scores:
        m_new = jnp.maximum(m_new, jnp.max(s, axis=1, keepdims=True))
    alpha = jnp.exp(m_prev - m_new)
    l_new = alpha * l_sc[:, :1]
    acc = alpha * acc_sc[...]
    for i in range(pages):
        pr = jnp.exp(scores[i] - m_new)
        l_new = l_new + jnp.sum(pr, axis=1, keepdims=True)
        acc = acc + _dot(pr.astype(BF16), vp_refs[i][...].astype(BF16))
    acc_sc[...] = acc
    l_sc[...] = jnp.broadcast_to(l_new, l_sc.shape)
    m_sc[...] = jnp.broadcast_to(m_new, m_sc.shape)
    tail_sc[...] = jnp.broadcast_to(tail, tail_sc.shape)

    @pl.when(g == n_groups - 1)
    def _():
        o_ref[...] = acc / l_new


def _fox_decode(page_table, q8, kn8, vn8, ff_col, bf_col, k_pool, v_pool, lf_pool, sf_pool, layer):
    b, n_pages = page_table.shape
    width = PAGE_SIZE * FOX_HEADS
    pages = _pick_tile(n_pages, (FOX_PAGES_PER_STEP, 2, 1))

    def page(i):
        return lambda bi, g, pt: (layer, pt[bi, n_pages - 1 - (g * pages + i)], 0, 0)

    tok = pl.BlockSpec((None, FOX_HEADS, FOX_DH), lambda bi, g, pt: (bi, 0, 0))
    col = pl.BlockSpec((None, FOX_HEADS, 1), lambda bi, g, pt: (bi, 0, 0))
    kv_specs = [pl.BlockSpec((None, None, width, FOX_DH), page(i)) for i in range(pages)]
    row_specs = [pl.BlockSpec((None, None, 1, width), page(i)) for i in range(pages)]
    grid_spec = pltpu.PrefetchScalarGridSpec(
        num_scalar_prefetch=1,
        grid=(b, n_pages // pages),
        in_specs=[tok, tok, tok, col, pl.BlockSpec((FOX_HEADS, 1), lambda bi, g, pt: (0, 0))]
        + kv_specs + kv_specs + row_specs + row_specs,
        out_specs=[tok, col],
        scratch_shapes=[
            pltpu.VMEM((FOX_HEADS, LANES), F32),
            pltpu.VMEM((FOX_HEADS, LANES), F32),
            pltpu.VMEM((FOX_HEADS, LANES), F32),
            pltpu.VMEM((FOX_HEADS, FOX_DH), F32),
        ],
    )
    return pl.pallas_call(
        functools.partial(_fox_decode_body, pages=pages),
        grid_spec=grid_spec,
        out_shape=[
            jax.ShapeDtypeStruct((b, FOX_HEADS, FOX_DH), F32),
            jax.ShapeDtypeStruct((b, FOX_HEADS, 1), F32),
        ],
        compiler_params=_cparams(("parallel", "arbitrary")),
        name="fox_decode",
    )(page_table, q8, kn8, vn8, ff_col, bf_col, *([k_pool] * pages), *([v_pool] * pages),
      *([lf_pool] * pages), *([sf_pool] * pages))


def _merge_body(o0, o1, o2, o3, g0, g1, g2, g3, w_ref, out_ref):
    acc = None
    for i, (o_ref, g_ref) in enumerate(((o0, g0), (o1, g1), (o2, g2), (o3, g3))):
        y = _sigmoid(g_ref[...]) * _dot(o_ref[...], w_ref[i])
        acc = y if acc is None else acc + y
    out_ref[...] = acc.astype(out_ref.dtype)


def _merge(branches, z, w_branch):
    m = z.shape[0]
    tm = _pick_tile(m, (512,))
    tn = 512
    assert D_MODEL % tn == 0 and OFF_GT % tn == 0
    o_spec = pl.BlockSpec((tm, BRANCH_W), lambda i, j: (i, 0))
    g_spec = lambda br: pl.BlockSpec((tm, tn), lambda i, j: (i, (OFF_GT + br * D_MODEL) // tn + j))
    return pl.pallas_call(
        _merge_body,
        grid=(m // tm, D_MODEL // tn),
        in_specs=[o_spec] * N_BRANCH + [g_spec(br) for br in range(N_BRANCH)]
        + [pl.BlockSpec((N_BRANCH, BRANCH_W, tn), lambda i, j: (0, 0, j))],
        out_specs=pl.BlockSpec((tm, tn), lambda i, j: (i, j)),
        out_shape=jax.ShapeDtypeStruct((m, D_MODEL), BF16),
        compiler_params=_cparams(("parallel", "arbitrary")),
        name="merge",
    )(*branches, z, z, z, z, w_branch)


def _pack_w_in(w):
    sizes = (FOX_W, FOX_W, FOX_W, FOX_HEADS, GLA_KW, GLA_KW, GLA_VW, GLA_VW, GLA_RANK,
             RET_W, RET_W, RET_W, RET_W, MEM_W, N_BRANCH * D_MODEL)
    cuts = np.concatenate([[0], np.cumsum(sizes)])
    seg = [w[:, int(cuts[i]):int(cuts[i + 1])] for i in range(len(sizes))]
    fq, fk, fv, ff, gq, gk, gv, gr, ga, rq, rk, rv, rg, mq, gt = seg
    pad = jnp.zeros((w.shape[0], LANES - FOX_HEADS - GLA_RANK), w.dtype)
    return jnp.concatenate([fq, fk, fv, gq, gk, gv, gr, rq, rk, rv, rg, mq, gt, ff, ga, pad], axis=1).astype(BF16)


def _pack_w_gla_a2(w2):
    out = jnp.zeros((LANES, GLA_KW), F32).at[SM_GA:SM_GA + GLA_RANK].set(w2)
    return out.astype(BF16)


def _rope_tables(pos):
    half = RET_DH // 2
    inv = ROPE_BASE ** (-jnp.arange(half, dtype=F32) / half)
    ang = pos.astype(F32)[:, None] * inv[None, :]
    return jnp.cos(ang), jnp.sin(ang)


def _retention_log_decay_table():
    lg = jnp.log1p(-jnp.exp2(-5.0 - jnp.arange(RET_HEADS, dtype=F32)))
    return jnp.broadcast_to(lg[:, None, None], (RET_HEADS, 1, LANES))


def _dense_tail(x, branches, z, lw):
    merged = _merge(branches, z, lw["w_branch"])
    x = _matmul(merged, lw["w_out"], res=x, name="out_proj")
    u = _matmul(x, lw["w_ff1"], g=lw["g_mlp"], act="relu2", out_dtype=BF16, name="ff1")
    return _matmul(u, lw["w_ff2"], res=x, name="ff2")


def kernel(x_prompt, x_sample, cache_fox_k, cache_fox_v, cache_fox_logf, state_gla, state_ret, cache_mem_k, cache_mem_v, page_table, mem_prompt, g_mix, w_in, b_fox_f, w_gla_a2, b_gla_a, g_gla, g_ret, w_mem_kv, w_branch, w_out, g_mlp, w_ff1, w_ff2, g_final):
    depth = w_in.shape[0]
    bp, t, d = x_prompt.shape
    bs, ts, _ = x_sample.shape
    assert ts == 1 and d == D_MODEL
    n_pool = cache_fox_k.shape[1]
    n_mem = mem_prompt.shape[1]
    past = page_table.shape[1] * PAGE_SIZE
    rows_s = -(-bs // BF16_ROWS) * BF16_ROWS

    cos_p, sin_p = _rope_tables(jnp.arange(t))
    cos_s, sin_s = _rope_tables(past + jnp.arange(ts))
    lg_tab = _retention_log_decay_table()
    k_pool = cache_fox_k.reshape(depth, n_pool, PAGE_SIZE * FOX_HEADS, FOX_DH)
    v_pool = cache_fox_v.reshape(depth, n_pool, PAGE_SIZE * FOX_HEADS, FOX_DH)
    lf_pages = cache_fox_logf.reshape(depth * n_pool, PAGE_SIZE * FOX_HEADS)
    lf_pool = lf_pages.reshape(depth, n_pool, 1, PAGE_SIZE * FOX_HEADS)
    sf_pool = _page_suffix(lf_pages).reshape(depth, n_pool, 1, PAGE_SIZE * FOX_HEADS)
    mem_k_s = cache_mem_k.reshape(depth, bs, n_mem, MEM_W)
    mem_v_s = cache_mem_v.reshape(depth, bs, n_mem, MEM_W)
    mem_prompt2 = mem_prompt.reshape(bp * n_mem, d)

    xp = x_prompt.reshape(bp * t, d)
    xs = jnp.zeros((rows_s, d), F32).at[:bs].set(x_sample.reshape(bs, d))
    outs = {k: [] for k in ("fk_p", "fv_p", "fl_p", "sg_p", "sr_p", "mk_p", "mv_p", "fk_s", "fv_s", "fl_s", "sg_s", "sr_s")}
    to_bf16_rows = lambda a: jnp.zeros((rows_s, a.size // bs), BF16).at[:bs].set(a.reshape(bs, -1).astype(BF16))

    for l in range(depth):
        lw = dict(
            w_branch=w_branch[l].astype(BF16), w_out=w_out[l].astype(BF16), g_mlp=g_mlp[l],
            w_ff1=w_ff1[l].astype(BF16), w_ff2=w_ff2[l].astype(BF16))
        w_in_p = _pack_w_in(w_in[l])
        w2p = _pack_w_gla_a2(w_gla_a2[l])

        z, zb = _matmul(xp, w_in_p, g=g_mix[l], bf16_copy=True, name="in_proj")
        z3 = z.reshape(bp, t, N_PACK)
        zb3 = zb.reshape(bp, t, N_PACK)
        ff_t = jnp.swapaxes(z3[:, :, OFF_SM + SM_FF:OFF_SM + SM_FF + FOX_HEADS], 1, 2)
        lf_t, c_t = _fox_gate(ff_t, b_fox_f[l])
        o_fox = _fox_attn(z3, zb3, c_t)
        mkv = _matmul(mem_prompt2, w_mem_kv[l].astype(BF16), name="mem_kv")
        mkv4 = mkv.reshape(1, bp, n_mem, 2 * MEM_W)
        o_mem = _mem_attn(z3, mkv4, mkv4, layer=0, kcol=0, vcol=1, out_dtype=BF16)
        o_gla, s_gla = _gla_prompt(z3, w2p, b_gla_a[l], g_gla[l])
        o_ret, s_ret = _ret_prompt(z3, cos_p, sin_p, lg_tab, g_ret[l])
        flat = lambda a: a.reshape(bp * t, -1)
        xp = _dense_tail(xp, [flat(o_fox), flat(o_gla), flat(o_ret), flat(o_mem)], z, lw)
        outs["fk_p"].append(z3[:, :, OFF_FK:OFF_FK + FOX_W].reshape(bp, t, FOX_HEADS, FOX_DH))
        outs["fv_p"].append(z3[:, :, OFF_FV:OFF_FV + FOX_W].reshape(bp, t, FOX_HEADS, FOX_DH))
        outs["fl_p"].append(jnp.swapaxes(lf_t, 1, 2))
        outs["sg_p"].append(s_gla)
        outs["sr_p"].append(s_ret)
        outs["mk_p"].append(mkv4[0, :, :, :MEM_W].reshape(bp, n_mem, MEM_HEADS, MEM_DH))
        outs["mv_p"].append(mkv4[0, :, :, MEM_W:].reshape(bp, n_mem, MEM_HEADS, MEM_DH))

        zs = _matmul(xs, w_in_p, g=g_mix[l], name="in_proj_s")
        zs3 = zs[:bs].reshape(bs, 1, N_PACK)
        ff_col = zs[:bs, OFF_SM + SM_FF:OFF_SM + SM_FF + FOX_HEADS].reshape(bs, FOX_HEADS, 1)
        heads = lambda off: zs[:bs, off:off + FOX_W].reshape(bs, FOX_HEADS, FOX_DH)
        o_fox_s, lf_s = _fox_decode(page_table, heads(OFF_FQ), heads(OFF_FK), heads(OFF_FV), ff_col,
                                    b_fox_f[l].reshape(FOX_HEADS, 1), k_pool, v_pool, lf_pool, sf_pool, l)
        o_mem_s = _mem_attn(zs3, mem_k_s, mem_v_s, layer=l, kcol=0, vcol=0, out_dtype=F32)
        o_gla_s, sg_s = _gla_step(zs3, w2p, b_gla_a[l], g_gla[l], state_gla, l)
        o_ret_s, sr_s = _ret_step(zs3, cos_s, sin_s, lg_tab, g_ret[l], state_ret, l)
        xs = _dense_tail(xs, [to_bf16_rows(a) for a in (o_fox_s, o_gla_s, o_ret_s, o_mem_s)], zs, lw)
        outs["fk_s"].append(zs3[:, :, OFF_FK:OFF_FK + FOX_W].reshape(bs, 1, FOX_HEADS, FOX_DH))
        outs["fv_s"].append(zs3[:, :, OFF_FV:OFF_FV + FOX_W].reshape(bs, 1, FOX_HEADS, FOX_DH))
        outs["fl_s"].append(lf_s.reshape(bs, 1, FOX_HEADS))
        outs["sg_s"].append(sg_s)
        outs["sr_s"].append(sr_s)

    y_prompt = _rmsnorm(xp, g_final).reshape(bp, t, d)
    y_sample = _rmsnorm(xs, g_final)[:bs].reshape(bs, 1, d)
    st = lambda k: jnp.stack(outs[k])
    return (y_prompt, y_sample, st("fk_p"), st("fv_p"), st("fl_p"), st("sg_p"), st("sr_p"), st("mk_p"), st("mv_p"),
            st("fk_s"), st("fv_s"), st("fl_s"), st("sg_s"), st("sr_s"))
```

```python
import functools

import numpy as np
import jax
import jax.numpy as jnp
from jax import lax
from jax.experimental import pallas as pl
from jax.experimental.pallas import tpu as pltpu

F32 = jnp.float32
BF16 = jnp.bfloat16

D_MODEL = 2048
FOX_HEADS = 8
FOX_DH = D_MODEL // 16
FOX_W = FOX_HEADS * FOX_DH
GLA_HEADS = 4
GLA_DK = D_MODEL // 16
GLA_DV = D_MODEL // 8
GLA_KW = GLA_HEADS * GLA_DK
GLA_VW = GLA_HEADS * GLA_DV
GLA_RANK = 16
GLA_TAU = 16.0
RET_HEADS = 4
RET_DH = D_MODEL // 8
RET_W = RET_HEADS * RET_DH
MEM_HEADS = 4
MEM_DH = D_MODEL // 8
MEM_W = MEM_HEADS * MEM_DH
N_BRANCH = 4
BRANCH_W = D_MODEL // 2
D_FF = 4 * D_MODEL
PAGE_SIZE = 128
ROPE_BASE = 10000.0
EPS = 1e-6
LOG2E = 1.4426950408889634

LANES = 128
SUBLANES = 8
BF16_ROWS = 16
VMEM_LIMIT_BYTES = 56 * 1024 * 1024

OFF_FQ = 0
OFF_FK = OFF_FQ + FOX_W
OFF_FV = OFF_FK + FOX_W
OFF_GQ = OFF_FV + FOX_W
OFF_GK = OFF_GQ + GLA_KW
OFF_GV = OFF_GK + GLA_KW
OFF_GR = OFF_GV + GLA_VW
OFF_RQ = OFF_GR + GLA_VW
OFF_RK = OFF_RQ + RET_W
OFF_RV = OFF_RK + RET_W
OFF_RG = OFF_RV + RET_W
OFF_MQ = OFF_RG + RET_W
OFF_GT = OFF_MQ + MEM_W
OFF_SM = OFF_GT + N_BRANCH * D_MODEL
SM_FF = 0
SM_GA = FOX_HEADS
N_PACK = OFF_SM + LANES

GLA_CHUNK = 64
GLA_SUB = 8
GLA_SUB_LOG2 = 3
GLA_CHUNKS_PER_STEP = 4
FOX_PAGES_PER_STEP = 8
FOX_DH_LOG2 = 7
assert GLA_SUB == 1 << GLA_SUB_LOG2 and FOX_DH == 1 << FOX_DH_LOG2 and LANES & (LANES - 1) == 0
RET_CHUNK = 256


def _cparams(sem):
    return pltpu.CompilerParams(dimension_semantics=sem, vmem_limit_bytes=VMEM_LIMIT_BYTES)


def _log_sigmoid(x):
    return jnp.minimum(x, 0.0) - jnp.log1p(jnp.exp(-jnp.abs(x)))


def _sigmoid(x):
    return 1.0 / (1.0 + jnp.exp(-x))


def _dot(a, b):
    return jnp.dot(a, b, preferred_element_type=F32)


def _dot_nt(a, b):
    return lax.dot_general(a, b, (((1,), (1,)), ((), ())), preferred_element_type=F32)


def _dot_tn(a, b):
    return lax.dot_general(a, b, (((0,), (0,)), ((), ())), preferred_element_type=F32)


def _dot01(m01, x):
    hi = x.astype(BF16)
    r1 = x - hi.astype(F32)
    mid = r1.astype(BF16)
    lo = (r1 - mid.astype(F32)).astype(BF16)
    return _dot(m01, hi) + _dot(m01, mid) + _dot(m01, lo)


def _row_to_cols(row):
    return jnp.broadcast_to(row, (LANES, LANES)).T


def _head_norm_gate(o, gain, gate):
    mu = jnp.mean(o, axis=1, keepdims=True)
    d = o - mu
    var = jnp.mean(d * d, axis=1, keepdims=True)
    y = d * lax.rsqrt(var + EPS) * gain
    return y * (gate * _sigmoid(gate))


def _mm_body(*refs, norm, act, has_res, stage, bf16_copy):
    it = iter(refs)
    x_ref = next(it)
    g_ref = next(it) if norm else None
    w_ref = next(it)
    r_ref = next(it) if has_res else None
    o_ref = next(it)
    ob_ref = next(it) if bf16_copy else None
    h_ref = next(it) if stage else None
    if stage:
        @pl.when(pl.program_id(1) == 0)
        def _():
            x = x_ref[...].astype(F32)
            if norm:
                ms = jnp.mean(x * x, axis=-1, keepdims=True)
                x = x * lax.rsqrt(ms + EPS) * g_ref[...]
            h_ref[...] = x.astype(BF16)

        h = h_ref[...]
    else:
        h = x_ref[...]
    acc = _dot(h, w_ref[...])
    if act == "relu2":
        acc = jnp.square(jnp.maximum(acc, 0.0))
    if has_res:
        acc = acc + r_ref[...]
    o_ref[...] = acc.astype(o_ref.dtype)
    if bf16_copy:
        ob_ref[...] = acc.astype(BF16)


def _pick_tile(n, candidates):
    for c in candidates:
        if n % c == 0:
            return c
    return n


def _matmul(x, w, *, g=None, res=None, act=None, out_dtype=F32, bf16_copy=False, name="proj"):
    m, k = x.shape
    n = w.shape[1]
    norm = g is not None
    stage = norm or x.dtype != BF16
    if m >= 1024:
        tm = 1024 if k <= 2048 else 512
    else:
        tm = m
    tn = _pick_tile(n, (1152, 1024, 512))
    if k > 2048:
        tn = min(tn, 512)
    assert m % tm == 0 and n % tn == 0 and tm % BF16_ROWS == 0
    in_specs = [pl.BlockSpec((tm, k), lambda i, j: (i, 0))]
    args = [x]
    if norm:
        in_specs.append(pl.BlockSpec((1, k), lambda i, j: (0, 0)))
        args.append(g.reshape(1, k).astype(F32))
    in_specs.append(pl.BlockSpec((k, tn), lambda i, j: (0, j)))
    args.append(w)
    if res is not None:
        in_specs.append(pl.BlockSpec((tm, tn), lambda i, j: (i, j)))
        args.append(res)
    scratch = [pltpu.VMEM((tm, k), BF16)] if stage else []
    o_spec = pl.BlockSpec((tm, tn), lambda i, j: (i, j))
    o_shape = jax.ShapeDtypeStruct((m, n), out_dtype)
    return pl.pallas_call(
        functools.partial(_mm_body, norm=norm, act=act, has_res=res is not None, stage=stage, bf16_copy=bf16_copy),
        grid=(m // tm, n // tn),
        in_specs=in_specs,
        out_specs=[o_spec, o_spec] if bf16_copy else o_spec,
        out_shape=[o_shape, jax.ShapeDtypeStruct((m, n), BF16)] if bf16_copy else o_shape,
        scratch_shapes=scratch,
        compiler_params=_cparams(("parallel", "arbitrary")),
        name=name,
    )(*args)


def _rmsnorm_body(x_ref, g_ref, o_ref):
    x = x_ref[...]
    ms = jnp.mean(x * x, axis=-1, keepdims=True)
    o_ref[...] = x * lax.rsqrt(ms + EPS) * g_ref[...]


def _rmsnorm(x, g):
    m, k = x.shape
    tm = _pick_tile(m, (512,))
    return pl.pallas_call(
        _rmsnorm_body,
        grid=(m // tm,),
        in_specs=[pl.BlockSpec((tm, k), lambda i: (i, 0)), pl.BlockSpec((1, k), lambda i: (0, 0))],
        out_specs=pl.BlockSpec((tm, k), lambda i: (i, 0)),
        out_shape=jax.ShapeDtypeStruct((m, k), F32),
        compiler_params=_cparams(("parallel",)),
        name="final_norm",
    )(x, g.reshape(1, k))


def _fox_gate_body(x_ref, b_ref, lf_ref, c_ref):
    lf = _log_sigmoid(x_ref[...] + b_ref[...])
    lf_ref[...] = lf
    t = lf.shape[1]
    lane = lax.broadcasted_iota(jnp.int32, lf.shape, 1)
    c = lf
    shift = 1
    while shift < t:
        c = c + jnp.where(lane >= shift, pltpu.roll(c, shift, axis=1), 0.0)
        shift *= 2
    c_ref[...] = c


def _fox_gate(ff_t, bias):
    b, h, t = ff_t.shape
    spec = pl.BlockSpec((None, h, t), lambda i: (i, 0, 0))
    return pl.pallas_call(
        _fox_gate_body,
        grid=(b,),
        in_specs=[spec, pl.BlockSpec((h, 1), lambda i: (0, 0))],
        out_specs=[spec, spec],
        out_shape=[jax.ShapeDtypeStruct((b, h, t), F32)] * 2,
        compiler_params=_cparams(("parallel",)),
        name="fox_gate",
    )(ff_t, bias.reshape(h, 1))


def _fox_attn_body(q_ref, k_ref, v_ref, ck_ref, o_ref, q_sc, m_sc, l_sc, acc_sc, *, tile):
    qi = pl.program_id(1)
    ki = pl.program_id(2)
    nk = pl.num_programs(2)

    @pl.when(ki == 0)
    def _():
        q_sc[...] = (q_ref[...] * (FOX_DH ** -0.5 * LOG2E)).astype(BF16)
        m_sc[...] = jnp.full(m_sc.shape, -jnp.inf, F32)
        l_sc[...] = jnp.zeros(l_sc.shape, F32)
        acc_sc[...] = jnp.zeros(acc_sc.shape, F32)

    def step(on_diagonal):
        ck = ck_ref[...] * LOG2E
        if on_diagonal:
            row = lax.broadcasted_iota(jnp.int32, (tile, tile), 0)
            col = lax.broadcasted_iota(jnp.int32, (tile, tile), 1)
            visible = col <= row
        for h in range(FOX_HEADS):
            sl = slice(h * FOX_DH, (h + 1) * FOX_DH)
            s = _dot_nt(q_sc[:, sl], k_ref[:, sl]) - ck[h:h + 1, :]
            if on_diagonal:
                s = jnp.where(visible, s, -jnp.inf)
            m_prev = m_sc[h]
            m_new = jnp.maximum(m_prev, jnp.max(s, axis=1, keepdims=True))
            alpha = jnp.exp2(m_prev - m_new)
            p = jnp.exp2(s - m_new[:, :1])
            l_sc[h] = alpha * l_sc[h] + jnp.sum(p, axis=1, keepdims=True)
            acc_sc[:, sl] = alpha * acc_sc[:, sl] + _dot(p.astype(BF16), v_ref[:, sl])
            m_sc[h] = m_new

    pl.when(ki < qi)(functools.partial(step, False))
    pl.when(ki == qi)(functools.partial(step, True))

    @pl.when(ki == nk - 1)
    def _():
        for h in range(FOX_HEADS):
            sl = slice(h * FOX_DH, (h + 1) * FOX_DH)
            o_ref[:, sl] = (acc_sc[:, sl] / l_sc[h]).astype(o_ref.dtype)


def _fox_attn(z3, zb3, c_row):
    b, t, _ = z3.shape
    tile = _pick_tile(t, (512, 256, 128))
    n = t // tile
    qcol, kcol, vcol = OFF_FQ // FOX_W, OFF_FK // FOX_W, OFF_FV // FOX_W
    return pl.pallas_call(
        functools.partial(_fox_attn_body, tile=tile),
        grid=(b, n, n),
        in_specs=[
            pl.BlockSpec((None, tile, FOX_W), lambda bi, qi, ki: (bi, qi, qcol)),
            pl.BlockSpec((None, tile, FOX_W), lambda bi, qi, ki: (bi, jnp.minimum(ki, qi), kcol)),
            pl.BlockSpec((None, tile, FOX_W), lambda bi, qi, ki: (bi, jnp.minimum(ki, qi), vcol)),
            pl.BlockSpec((None, FOX_HEADS, tile), lambda bi, qi, ki: (bi, 0, jnp.minimum(ki, qi))),
        ],
        out_specs=pl.BlockSpec((None, tile, FOX_W), lambda bi, qi, ki: (bi, qi, 0)),
        out_shape=jax.ShapeDtypeStruct((b, t, FOX_W), BF16),
        scratch_shapes=[
            pltpu.VMEM((tile, FOX_W), BF16),
            pltpu.VMEM((FOX_HEADS, tile, FOX_DH), F32),
            pltpu.VMEM((FOX_HEADS, tile, FOX_DH), F32),
            pltpu.VMEM((tile, FOX_W), F32),
        ],
        compiler_params=_cparams(("parallel", "parallel", "arbitrary")),
        name="fox_attn",
    )(z3, zb3, zb3, c_row)


def _mem_attn_body(q_ref, k_ref, v_ref, o_ref, *, rows):
    scale = MEM_DH ** -0.5
    for h in range(MEM_HEADS):
        sl = slice(h * MEM_DH, (h + 1) * MEM_DH)
        q = q_ref[:, sl]
        if rows < SUBLANES:
            q = jnp.broadcast_to(q, (SUBLANES, MEM_DH))
        s = _dot_nt(q.astype(BF16), k_ref[:, sl].astype(BF16)) * scale
        m = jnp.max(s, axis=1, keepdims=True)
        p = jnp.exp(s - m)
        p = p / jnp.sum(p, axis=1, keepdims=True)
        o = _dot(p.astype(BF16), v_ref[:, sl].astype(BF16))
        o_ref[:, sl] = o[:rows].astype(o_ref.dtype)


def _mem_attn(z3, k4, v4, *, layer, kcol, vcol, out_dtype):
    b, t, _ = z3.shape
    n_mem = k4.shape[2]
    tq = _pick_tile(t, (512, 256, 128))
    qcol = OFF_MQ // MEM_W
    return pl.pallas_call(
        functools.partial(_mem_attn_body, rows=tq),
        grid=(b, t // tq),
        in_specs=[
            pl.BlockSpec((None, tq, MEM_W), lambda bi, qi: (bi, qi, qcol)),
            pl.BlockSpec((None, None, n_mem, MEM_W), lambda bi, qi: (layer, bi, 0, kcol)),
            pl.BlockSpec((None, None, n_mem, MEM_W), lambda bi, qi: (layer, bi, 0, vcol)),
        ],
        out_specs=pl.BlockSpec((None, tq, MEM_W), lambda bi, qi: (bi, qi, 0)),
        out_shape=jax.ShapeDtypeStruct((b, t, MEM_W), out_dtype),
        compiler_params=_cparams(("parallel", "parallel")),
        name="mem_attn",
    )(z3, k4, v4)


def _rotary(x, cos, sin):
    half = x.shape[1] // 2
    x1, x2 = x[:, :half], x[:, half:]
    return jnp.concatenate([x1 * cos - x2 * sin, x1 * sin + x2 * cos], axis=1)


def _ret_body(q_ref, k_ref, v_ref, g_ref, cos_ref, sin_ref, lg_ref, gn_ref, o_ref, s_ref, st_sc, *, chunk):
    c = pl.program_id(1)
    nc = pl.num_programs(1)

    @pl.when(c == 0)
    def _():
        st_sc[...] = jnp.zeros(st_sc.shape, F32)

    cos, sin = cos_ref[...], sin_ref[...]
    row = lax.broadcasted_iota(jnp.int32, (chunk, chunk), 0)
    col = lax.broadcasted_iota(jnp.int32, (chunk, chunk), 1)
    lag = (row - col).astype(F32)
    t = lax.broadcasted_iota(jnp.int32, (chunk, 1), 0).astype(F32)
    for h in range(RET_HEADS):
        sl = slice(h * RET_DH, (h + 1) * RET_DH)
        lg = lg_ref[h][:, :1]
        q = _rotary(q_ref[:, sl], cos, sin)
        k = _rotary(k_ref[:, sl], cos, sin) * (RET_DH ** -0.5)
        v = v_ref[:, sl].astype(BF16)
        decay = jnp.where(col <= row, jnp.exp(lag * lg), 0.0)
        a = _dot_nt(q.astype(BF16), k.astype(BF16)) * decay
        s0 = st_sc[h]
        o = _dot(a.astype(BF16), v) + jnp.exp((t + 1.0) * lg) * _dot(q.astype(BF16), s0.astype(BF16))
        k_end = k * jnp.exp((chunk - 1.0 - t) * lg)
        s_new = jnp.exp(chunk * lg) * s0 + _dot_tn(k_end.astype(BF16), v)
        st_sc[h] = s_new
        o_ref[:, sl] = _head_norm_gate(o, gn_ref[:, sl], g_ref[:, sl]).astype(o_ref.dtype)

        @pl.when(c == nc - 1)
        def _():
            s_ref[h] = s_new


def _ret_prompt(z3, cos, sin, lg_tab, g_ret):
    b, t, _ = z3.shape
    chunk = _pick_tile(t, (RET_CHUNK, 128, 64))
    zspec = lambda off: pl.BlockSpec((None, chunk, RET_W), lambda bi, c: (bi, c, off // RET_W))
    return pl.pallas_call(
        functools.partial(_ret_body, chunk=chunk),
        grid=(b, t // chunk),
        in_specs=[
            zspec(OFF_RQ), zspec(OFF_RK), zspec(OFF_RV), zspec(OFF_RG),
            pl.BlockSpec((chunk, RET_DH // 2), lambda bi, c: (c, 0)),
            pl.BlockSpec((chunk, RET_DH // 2), lambda bi, c: (c, 0)),
            pl.BlockSpec((RET_HEADS, 1, LANES), lambda bi, c: (0, 0, 0)),
            pl.BlockSpec((1, RET_W), lambda bi, c: (0, 0)),
        ],
        out_specs=[
            pl.BlockSpec((None, chunk, RET_W), lambda bi, c: (bi, c, 0)),
            pl.BlockSpec((None, RET_HEADS, RET_DH, RET_DH), lambda bi, c: (bi, 0, 0, 0)),
        ],
        out_shape=[
            jax.ShapeDtypeStruct((b, t, RET_W), BF16),
            jax.ShapeDtypeStruct((b, RET_HEADS, RET_DH, RET_DH), F32),
        ],
        scratch_shapes=[pltpu.VMEM((RET_HEADS, RET_DH, RET_DH), F32)],
        compiler_params=_cparams(("parallel", "arbitrary")),
        name="ret_prompt",
    )(z3, z3, z3, z3, cos, sin, lg_tab, g_ret.reshape(1, RET_W))


def _gla_loga(zs, w2_ref, ba_ref):
    pre = _dot(zs.astype(BF16), w2_ref[...]) + ba_ref[...]
    return _log_sigmoid(pre) * (1.0 / GLA_TAU)


def _gla_chunk(q, k, v_ref, gr_ref, gn_ref, o_ref, loga, st, tri, hsum, rows):
    chunk = q.shape[0]
    n_sub = chunk // GLA_SUB
    b = _dot01(tri, loga)
    row = lax.broadcasted_iota(jnp.int32, (chunk, GLA_KW), 0)
    b_last = b[chunk - 1:chunk, :]
    beta = jnp.zeros_like(b)
    for i in range(1, n_sub):
        beta = jnp.where(row >= i * GLA_SUB, b[i * GLA_SUB - 1:i * GLA_SUB, :], beta)
    q_rel = q * jnp.exp(b - beta)
    q_abs = q * jnp.exp(b)
    k_end = k * jnp.exp(b_last - b)
    k_rel = []
    for i in range(1, n_sub):
        bi = b[i * GLA_SUB - 1:i * GLA_SUB, :]
        k_rel.append(jnp.where(row < i * GLA_SUB, k * jnp.exp(jnp.minimum(bi - b, 0.0)), 0.0).astype(BF16))

    prods = []
    for delta in range(GLA_SUB):
        kd = k if delta == 0 else pltpu.roll(k, delta, axis=0)
        bd = b if delta == 0 else pltpu.roll(b, delta, axis=0)
        prods.append((q * kd * jnp.exp(jnp.minimum(b - bd, 0.0))).astype(BF16))
    sums = _dot(jnp.concatenate(prods, axis=0), hsum)
    rowb = lax.broadcasted_iota(jnp.int32, (chunk, GLA_HEADS * LANES), 0)
    colb = lax.broadcasted_iota(jnp.int32, (chunk, GLA_HEADS * LANES), 1) & (LANES - 1)
    rsub = rowb & (GLA_SUB - 1)
    band = jnp.zeros((chunk, GLA_HEADS * LANES), F32)
    for delta in range(GLA_SUB):
        keep = jnp.logical_and(colb == rowb - delta, rsub >= delta)
        band = band + jnp.where(keep, sums[delta * chunk:(delta + 1) * chunk], 0.0)

    sub_of_row = row[:, :GLA_DK] >> GLA_SUB_LOG2
    new_st = []
    for h in range(GLA_HEADS):
        ks = slice(h * GLA_DK, (h + 1) * GLA_DK)
        vs = slice(h * GLA_DV, (h + 1) * GLA_DV)
        v = v_ref[rows, vs].astype(BF16)
        lhs = jnp.concatenate(
            [jnp.where(sub_of_row == i, q_rel[:, ks], 0.0) for i in range(1, n_sub)], axis=1).astype(BF16)
        rhs = jnp.concatenate([kr[:, ks] for kr in k_rel], axis=1)
        a = _dot_nt(lhs, rhs) + band[:, h * LANES:h * LANES + chunk]
        o = _dot(a.astype(BF16), v) + _dot_nt(q_abs[:, ks].astype(BF16), st[h].astype(BF16))
        new_st.append(st[h] * jnp.exp(b_last[:, ks]) + _dot_tn(v, k_end[:, ks].astype(BF16)))
        o_ref[rows, vs] = _head_norm_gate(o, gn_ref[:, vs], gr_ref[rows, vs]).astype(o_ref.dtype)
    return new_st


def _gla_body(q_ref, k_ref, v_ref, gr_ref, zs_ref, w2_ref, ba_ref, gn_ref, tri_ref, hsum_ref,
              o_ref, s_ref, st_sc, *, chunk, n_chunks):
    c = pl.program_id(1)
    nc = pl.num_programs(1)

    @pl.when(c == 0)
    def _():
        st_sc[...] = jnp.zeros(st_sc.shape, F32)

    loga = _gla_loga(zs_ref[...], w2_ref, ba_ref)
    tri, hsum = tri_ref[...], hsum_ref[...]
    st = [st_sc[h] for h in range(GLA_HEADS)]
    for ci in range(n_chunks):
        rows = slice(ci * chunk, (ci + 1) * chunk)
        q = q_ref[rows, :] * (GLA_DK ** -0.5)
        st = _gla_chunk(q, k_ref[rows, :], v_ref, gr_ref, gn_ref, o_ref, loga[rows], st, tri, hsum, rows)
    for h in range(GLA_HEADS):
        st_sc[h] = st[h]

    @pl.when(c == nc - 1)
    def _():
        for h in range(GLA_HEADS):
            s_ref[h] = st[h].T


def _gla_consts(chunk):
    r = np.arange(chunk)
    tri = (r[None, :] <= r[:, None]).astype(np.float32)
    lane = np.arange(GLA_KW)
    out = np.arange(GLA_HEADS * LANES)
    hsum = (lane[:, None] // GLA_DK == out[None, :] // LANES).astype(np.float32)
    return jnp.asarray(tri, BF16), jnp.asarray(hsum, BF16)


def _gla_prompt(z3, w2p, b_a, g_gla):
    b, t, _ = z3.shape
    chunk = GLA_CHUNK
    n_chunks = _pick_tile(t // chunk, (GLA_CHUNKS_PER_STEP, 2, 1))
    rows = chunk * n_chunks
    assert t % rows == 0
    tri, hsum = _gla_consts(chunk)
    full = lambda shape: pl.BlockSpec(shape, lambda bi, c: (0,) * len(shape))
    return pl.pallas_call(
        functools.partial(_gla_body, chunk=chunk, n_chunks=n_chunks),
        grid=(b, t // rows),
        in_specs=[
            pl.BlockSpec((None, rows, GLA_KW), lambda bi, c: (bi, c, OFF_GQ // GLA_KW)),
            pl.BlockSpec((None, rows, GLA_KW), lambda bi, c: (bi, c, OFF_GK // GLA_KW)),
            pl.BlockSpec((None, rows, GLA_VW), lambda bi, c: (bi, c, OFF_GV // GLA_VW)),
            pl.BlockSpec((None, rows, GLA_VW), lambda bi, c: (bi, c, OFF_GR // GLA_VW)),
            pl.BlockSpec((None, rows, LANES), lambda bi, c: (bi, c, OFF_SM // LANES)),
            full((LANES, GLA_KW)), full((1, GLA_KW)), full((1, GLA_VW)),
            full((chunk, chunk)), full((GLA_KW, GLA_HEADS * LANES)),
        ],
        out_specs=[
            pl.BlockSpec((None, rows, GLA_VW), lambda bi, c: (bi, c, 0)),
            pl.BlockSpec((None, GLA_HEADS, GLA_DK, GLA_DV), lambda bi, c: (bi, 0, 0, 0)),
        ],
        out_shape=[
            jax.ShapeDtypeStruct((b, t, GLA_VW), BF16),
            jax.ShapeDtypeStruct((b, GLA_HEADS, GLA_DK, GLA_DV), F32),
        ],
        scratch_shapes=[pltpu.VMEM((GLA_HEADS, GLA_DV, GLA_DK), F32)],
        compiler_params=_cparams(("parallel", "arbitrary")),
        name="gla_prompt",
    )(z3, z3, z3, z3, z3, w2p, b_a.reshape(1, GLA_KW), g_gla.reshape(1, GLA_VW), tri, hsum)


def _gla_step_body(q_ref, k_ref, v_ref, gr_ref, zs_ref, w2_ref, ba_ref, gn_ref, s0_ref, o_ref, s_ref):
    zs = jnp.broadcast_to(zs_ref[...], (BF16_ROWS, LANES))
    loga = _gla_loga(zs, w2_ref, ba_ref)[:1]
    q = q_ref[...] * (GLA_DK ** -0.5)
    k = k_ref[...]
    for h in range(GLA_HEADS):
        ks = slice(h * GLA_DK, (h + 1) * GLA_DK)
        vs = slice(h * GLA_DV, (h + 1) * GLA_DV)
        widen = lambda m: jnp.concatenate([m] * (GLA_DV // LANES), axis=1)
        a_c = widen(_row_to_cols(jnp.exp(loga[:, ks])))
        k_c = widen(_row_to_cols(k[:, ks]))
        q_c = widen(_row_to_cols(q[:, ks]))
        s_new = a_c * s0_ref[h] + k_c * v_ref[:, vs]
        s_ref[h] = s_new
        o = jnp.sum(q_c * s_new, axis=0, keepdims=True)
        o_ref[:, vs] = _head_norm_gate(o, gn_ref[:, vs], gr_ref[:, vs])


def _gla_step(zs3, w2p, b_a, g_gla, state, layer):
    b = zs3.shape[0]
    full = lambda shape: pl.BlockSpec(shape, lambda bi: (0,) * len(shape))
    return pl.pallas_call(
        _gla_step_body,
        grid=(b,),
        in_specs=[
            pl.BlockSpec((None, 1, GLA_KW), lambda bi: (bi, 0, OFF_GQ // GLA_KW)),
            pl.BlockSpec((None, 1, GLA_KW), lambda bi: (bi, 0, OFF_GK // GLA_KW)),
            pl.BlockSpec((None, 1, GLA_VW), lambda bi: (bi, 0, OFF_GV // GLA_VW)),
            pl.BlockSpec((None, 1, GLA_VW), lambda bi: (bi, 0, OFF_GR // GLA_VW)),
            pl.BlockSpec((None, 1, LANES), lambda bi: (bi, 0, OFF_SM // LANES)),
            full((LANES, GLA_KW)), full((1, GLA_KW)), full((1, GLA_VW)),
            pl.BlockSpec((None, None, GLA_HEADS, GLA_DK, GLA_DV), lambda bi: (layer, bi, 0, 0, 0)),
        ],
        out_specs=[
            pl.BlockSpec((None, 1, GLA_VW), lambda bi: (bi, 0, 0)),
            pl.BlockSpec((None, GLA_HEADS, GLA_DK, GLA_DV), lambda bi: (bi, 0, 0, 0)),
        ],
        out_shape=[
            jax.ShapeDtypeStruct((b, 1, GLA_VW), F32),
            jax.ShapeDtypeStruct((b, GLA_HEADS, GLA_DK, GLA_DV), F32),
        ],
        compiler_params=_cparams(("parallel",)),
        name="gla_step",
    )(zs3, zs3, zs3, zs3, zs3, w2p, b_a.reshape(1, GLA_KW), g_gla.reshape(1, GLA_VW), state)


def _ret_step_body(q_ref, k_ref, v_ref, g_ref, cos_ref, sin_ref, lg_ref, gn_ref, s0_ref, o_ref, s_ref):
    cos, sin = cos_ref[...], sin_ref[...]
    half = RET_DH // 2
    for h in range(RET_HEADS):
        sl = slice(h * RET_DH, (h + 1) * RET_DH)
        q = _rotary(q_ref[:, sl], cos, sin)
        k = _rotary(k_ref[:, sl], cos, sin) * (RET_DH ** -0.5)
        cols = lambda r: jnp.concatenate(
            [jnp.concatenate([_row_to_cols(r[:, i * half:(i + 1) * half])] * 2, axis=1) for i in range(2)], axis=0)
        gamma = jnp.exp(lg_ref[h][:, :1])
        s_new = gamma * s0_ref[h] + cols(k) * v_ref[:, sl]
        s_ref[h] = s_new
        o = jnp.sum(cols(q) * s_new, axis=0, keepdims=True)
        o_ref[:, sl] = _head_norm_gate(o, gn_ref[:, sl], g_ref[:, sl])


def _ret_step(zs3, cos, sin, lg_tab, g_ret, state, layer):
    b = zs3.shape[0]
    full = lambda shape: pl.BlockSpec(shape, lambda bi: (0,) * len(shape))
    zspec = lambda off: pl.BlockSpec((None, 1, RET_W), lambda bi: (bi, 0, off // RET_W))
    return pl.pallas_call(
        _ret_step_body,
        grid=(b,),
        in_specs=[
            zspec(OFF_RQ), zspec(OFF_RK), zspec(OFF_RV), zspec(OFF_RG),
            full((1, RET_DH // 2)), full((1, RET_DH // 2)), full((RET_HEADS, 1, LANES)), full((1, RET_W)),
            pl.BlockSpec((None, None, RET_HEADS, RET_DH, RET_DH), lambda bi: (layer, bi, 0, 0, 0)),
        ],
        out_specs=[
            pl.BlockSpec((None, 1, RET_W), lambda bi: (bi, 0, 0)),
            pl.BlockSpec((None, RET_HEADS, RET_DH, RET_DH), lambda bi: (bi, 0, 0, 0)),
        ],
        out_shape=[
            jax.ShapeDtypeStruct((b, 1, RET_W), F32),
            jax.ShapeDtypeStruct((b, RET_HEADS, RET_DH, RET_DH), F32),
        ],
        compiler_params=_cparams(("parallel",)),
        name="ret_step",
    )(zs3, zs3, zs3, zs3, cos, sin, lg_tab, g_ret.reshape(1, RET_W), state)


def _page_suffix_body(x_ref, o_ref):
    x = x_ref[...]
    width = x.shape[1]
    lane = lax.broadcasted_iota(jnp.int32, x.shape, 1)
    incl = x
    shift = FOX_HEADS
    while shift < width:
        moved = pltpu.roll(incl, width - shift, axis=1)
        incl = incl + jnp.where(lane < width - shift, moved, 0.0)
        shift *= 2
    o_ref[...] = incl


def _page_suffix(lf_pages):
    n, width = lf_pages.shape
    rows = _pick_tile(n, (64, 32, 16, 8))
    spec = pl.BlockSpec((rows, width), lambda i: (i, 0))
    return pl.pallas_call(
        _page_suffix_body,
        grid=(n // rows,),
        in_specs=[spec],
        out_specs=spec,
        out_shape=jax.ShapeDtypeStruct((n, width), F32),
        compiler_params=_cparams(("parallel",)),
        name="page_suffix",
    )(lf_pages)


def _fox_decode_body(*refs, pages):
    q_ref, kn_ref, vn_ref, ff_ref, bf_ref = refs[1:6]
    kp_refs = refs[6:6 + pages]
    vp_refs = refs[6 + pages:6 + 2 * pages]
    lp_refs = refs[6 + 2 * pages:6 + 3 * pages]
    sf_refs = refs[6 + 3 * pages:6 + 4 * pages]
    o_ref, lf_ref, m_sc, l_sc, tail_sc, acc_sc = refs[6 + 4 * pages:]
    g = pl.program_id(1)
    n_groups = pl.num_programs(1)
    scale = FOX_DH ** -0.5
    width = PAGE_SIZE * FOX_HEADS
    sub = lax.broadcasted_iota(jnp.int32, (FOX_HEADS, width), 0)
    lane = lax.broadcasted_iota(jnp.int32, (FOX_HEADS, width), 1)
    own = (lane & (FOX_HEADS - 1)) == sub
    q = q_ref[...]
    qb = q.astype(BF16)

    @pl.when(g == 0)
    def _():
        lf_new = _log_sigmoid(ff_ref[...] + bf_ref[...])
        lf_ref[...] = lf_new
        tail_sc[...] = jnp.broadcast_to(lf_new, tail_sc.shape)
        s_new = jnp.sum(q * kn_ref[...], axis=1, keepdims=True) * scale
        m_sc[...] = jnp.broadcast_to(s_new, m_sc.shape)
        l_sc[...] = jnp.ones(l_sc.shape, F32)
        acc_sc[...] = vn_ref[...]

    tail = tail_sc[:, :1]
    scores = []
    for i in range(pages):
        incl = jnp.broadcast_to(sf_refs[i][...], (FOX_HEADS, width))
        bias = tail + (incl - lp_refs[i][...])
        s = _dot_nt(qb, kp_refs[i][...].astype(BF16)) * scale + bias
        scores.append(jnp.where(own, s, -jnp.inf))
        tail = tail + jnp.sum(jnp.where(lane == sub, incl, 0.0), axis=1, keepdims=True)
    m_prev = m_sc[:, :1]
    m_new = m_prev
    for s in scores:
        m_new = jnp.maximum(m_new, jnp.max(s, axis=1, keepdims=True))
    alpha = jnp.exp(m_prev - m_new)
    l_new = alpha * l_sc[:, :1]
    acc = alpha * acc_sc[...]
    for i in range(pages):
        pr = jnp.exp(scores[i] - m_new)
        l_new = l_new + jnp.sum(pr, axis=1, keepdims=True)
        acc = acc + _dot(pr.astype(BF16), vp_refs[i][...].astype(BF16))
    acc_sc[...] = acc
    l_sc[...] = jnp.broadcast_to(l_new, l_sc.shape)
    m_sc[...] = jnp.broadcast_to(m_new, m_sc.shape)
    tail_sc[...] = jnp.broadcast_to(tail, tail_sc.shape)

    @pl.when(g == n_groups - 1)
    def _():
        o_ref[...] = acc / l_new


def _fox_decode(page_table, q8, kn8, vn8, ff_col, bf_col, k_pool, v_pool, lf_pool, sf_pool, layer):
    b, n_pages = page_table.shape
    width = PAGE_SIZE * FOX_HEADS
    pages = _pick_tile(n_pages, (FOX_PAGES_PER_STEP, 2, 1))

    def page(i):
        return lambda bi, g, pt: (layer, pt[bi, n_pages - 1 - (g * pages + i)], 0, 0)

    tok = pl.BlockSpec((None, FOX_HEADS, FOX_DH), lambda bi, g, pt: (bi, 0, 0))
    col = pl.BlockSpec((None, FOX_HEADS, 1), lambda bi, g, pt: (bi, 0, 0))
    kv_specs = [pl.BlockSpec((None, None, width, FOX_DH), page(i)) for i in range(pages)]
    row_specs = [pl.BlockSpec((None, None, 1, width), page(i)) for i in range(pages)]
    grid_spec = pltpu.PrefetchScalarGridSpec(
        num_scalar_prefetch=1,
        grid=(b, n_pages // pages),
        in_specs=[tok, tok, tok, col, pl.BlockSpec((FOX_HEADS, 1), lambda bi, g, pt: (0, 0))]
        + kv_specs + kv_specs + row_specs + row_specs,
        out_specs=[tok, col],
        scratch_shapes=[
            pltpu.VMEM((FOX_HEADS, LANES), F32),
            pltpu.VMEM((FOX_HEADS, LANES), F32),
            pltpu.VMEM((FOX_HEADS, LANES), F32),
            pltpu.VMEM((FOX_HEADS, FOX_DH), F32),
        ],
    )
    return pl.pallas_call(
        functools.partial(_fox_decode_body, pages=pages),
        grid_spec=grid_spec,
        out_shape=[
            jax.ShapeDtypeStruct((b, FOX_HEADS, FOX_DH), F32),
            jax.ShapeDtypeStruct((b, FOX_HEADS, 1), F32),
        ],
        compiler_params=_cparams(("parallel", "arbitrary")),
        name="fox_decode",
    )(page_table, q8, kn8, vn8, ff_col, bf_col, *([k_pool] * pages), *([v_pool] * pages),
      *([lf_pool] * pages), *([sf_pool] * pages))


def _merge_body(o0, o1, o2, o3, g0, g1, g2, g3, w_ref, out_ref):
    acc = None
    for i, (o_ref, g_ref) in enumerate(((o0, g0), (o1, g1), (o2, g2), (o3, g3))):
        y = (0.5 * jnp.tanh(0.5 * g_ref[...]) + 0.5) * _dot(o_ref[...], w_ref[i])
        acc = y if acc is None else acc + y
    out_ref[...] = acc.astype(out_ref.dtype)


def _merge(branches, z, w_branch):
    m = z.shape[0]
    tm = _pick_tile(m, (512,))
    tn = 512
    assert D_MODEL % tn == 0 and OFF_GT % tn == 0
    o_spec = pl.BlockSpec((tm, BRANCH_W), lambda i, j: (i, 0))
    g_spec = lambda br: pl.BlockSpec((tm, tn), lambda i, j: (i, (OFF_GT + br * D_MODEL) // tn + j))
    return pl.pallas_call(
        _merge_body,
        grid=(m // tm, D_MODEL // tn),
        in_specs=[o_spec] * N_BRANCH + [g_spec(br) for br in range(N_BRANCH)]
        + [pl.BlockSpec((N_BRANCH, BRANCH_W, tn), lambda i, j: (0, 0, j))],
        out_specs=pl.BlockSpec((tm, tn), lambda i, j: (i, j)),
        out_shape=jax.ShapeDtypeStruct((m, D_MODEL), BF16),
        compiler_params=_cparams(("parallel", "arbitrary")),
        name="merge",
    )(*branches, z, z, z, z, w_branch)


def _pack_w_in(w):
    c_ff = 3 * FOX_W
    c_ga = c_ff + FOX_HEADS + 2 * GLA_KW + 2 * GLA_VW
    pad = jnp.zeros((w.shape[0], LANES - FOX_HEADS - GLA_RANK), w.dtype)
    return jnp.concatenate(
        [w[:, :c_ff], w[:, c_ff + FOX_HEADS:c_ga], w[:, c_ga + GLA_RANK:],
         w[:, c_ff:c_ff + FOX_HEADS], w[:, c_ga:c_ga + GLA_RANK], pad], axis=1).astype(BF16)


def _pack_w_gla_a2(w2):
    out = jnp.zeros((LANES, GLA_KW), F32).at[SM_GA:SM_GA + GLA_RANK].set(w2)
    return out.astype(BF16)


def _rope_tables(pos):
    half = RET_DH // 2
    inv = ROPE_BASE ** (-jnp.arange(half, dtype=F32) / half)
    ang = pos.astype(F32)[:, None] * inv[None, :]
    return jnp.cos(ang), jnp.sin(ang)


def _retention_log_decay_table():
    lg = jnp.log1p(-jnp.exp2(-5.0 - jnp.arange(RET_HEADS, dtype=F32)))
    return jnp.broadcast_to(lg[:, None, None], (RET_HEADS, 1, LANES))


def _dense_tail(x, branches, z, lw):
    merged = _merge(branches, z, lw["w_branch"])
    x = _matmul(merged, lw["w_out"], res=x, name="out_proj")
    u = _matmul(x, lw["w_ff1"], g=lw["g_mlp"], act="relu2", out_dtype=BF16, name="ff1")
    return _matmul(u, lw["w_ff2"], res=x, name="ff2")


def kernel(x_prompt, x_sample, cache_fox_k, cache_fox_v, cache_fox_logf, state_gla, state_ret, cache_mem_k, cache_mem_v, page_table, mem_prompt, g_mix, w_in, b_fox_f, w_gla_a2, b_gla_a, g_gla, g_ret, w_mem_kv, w_branch, w_out, g_mlp, w_ff1, w_ff2, g_final):
    depth = w_in.shape[0]
    bp, t, d = x_prompt.shape
    bs, ts, _ = x_sample.shape
    assert ts == 1 and d == D_MODEL
    n_pool = cache_fox_k.shape[1]
    n_mem = mem_prompt.shape[1]
    past = page_table.shape[1] * PAGE_SIZE
    rows_s = -(-bs // BF16_ROWS) * BF16_ROWS

    cos_p, sin_p = _rope_tables(jnp.arange(t))
    cos_s, sin_s = _rope_tables(past + jnp.arange(ts))
    lg_tab = _retention_log_decay_table()
    k_pool = cache_fox_k.reshape(depth, n_pool, PAGE_SIZE * FOX_HEADS, FOX_DH)
    v_pool = cache_fox_v.reshape(depth, n_pool, PAGE_SIZE * FOX_HEADS, FOX_DH)
    lf_pages = cache_fox_logf.reshape(depth * n_pool, PAGE_SIZE * FOX_HEADS)
    lf_pool = lf_pages.reshape(depth, n_pool, 1, PAGE_SIZE * FOX_HEADS)
    sf_pool = _page_suffix(lf_pages).reshape(depth, n_pool, 1, PAGE_SIZE * FOX_HEADS)
    mem_k_s = cache_mem_k.reshape(depth, bs, n_mem, MEM_W)
    mem_v_s = cache_mem_v.reshape(depth, bs, n_mem, MEM_W)
    mem_prompt2 = mem_prompt.reshape(bp * n_mem, d)

    xp = x_prompt.reshape(bp * t, d)
    xs = jnp.zeros((rows_s, d), F32).at[:bs].set(x_sample.reshape(bs, d))
    outs = {k: [] for k in ("fk_p", "fv_p", "fl_p", "sg_p", "sr_p", "mk_p", "mv_p", "fk_s", "fv_s", "fl_s", "sg_s", "sr_s")}
    to_bf16_rows = lambda a: jnp.zeros((rows_s, a.size // bs), BF16).at[:bs].set(a.reshape(bs, -1).astype(BF16))

    for l in range(depth):
        lw = dict(
            w_branch=w_branch[l].astype(BF16), w_out=w_out[l].astype(BF16), g_mlp=g_mlp[l],
            w_ff1=w_ff1[l].astype(BF16), w_ff2=w_ff2[l].astype(BF16))
        w_in_p = _pack_w_in(w_in[l])
        w2p = _pack_w_gla_a2(w_gla_a2[l])

        z, zb = _matmul(xp, w_in_p, g=g_mix[l], bf16_copy=True, name="in_proj")
        z3 = z.reshape(bp, t, N_PACK)
        zb3 = zb.reshape(bp, t, N_PACK)
        ff_t = jnp.swapaxes(z3[:, :, OFF_SM + SM_FF:OFF_SM + SM_FF + FOX_HEADS], 1, 2)
        lf_t, c_t = _fox_gate(ff_t, b_fox_f[l])
        o_fox = _fox_attn(z3, zb3, c_t)
        mkv = _matmul(mem_prompt2, w_mem_kv[l].astype(BF16), name="mem_kv")
        mkv4 = mkv.reshape(1, bp, n_mem, 2 * MEM_W)
        o_mem = _mem_attn(z3, mkv4, mkv4, layer=0, kcol=0, vcol=1, out_dtype=BF16)
        o_gla, s_gla = _gla_prompt(z3, w2p, b_gla_a[l], g_gla[l])
        o_ret, s_ret = _ret_prompt(z3, cos_p, sin_p, lg_tab, g_ret[l])
        flat = lambda a: a.reshape(bp * t, -1)
        xp = _dense_tail(xp, [flat(o_fox), flat(o_gla), flat(o_ret), flat(o_mem)], z, lw)
        outs["fk_p"].append(z3[:, :, OFF_FK:OFF_FK + FOX_W].reshape(bp, t, FOX_HEADS, FOX_DH))
        outs["fv_p"].append(z3[:, :, OFF_FV:OFF_FV + FOX_W].reshape(bp, t, FOX_HEADS, FOX_DH))
        outs["fl_p"].append(jnp.swapaxes(lf_t, 1, 2))
        outs["sg_p"].append(s_gla)
        outs["sr_p"].append(s_ret)
        outs["mk_p"].append(mkv4[0, :, :, :MEM_W].reshape(bp, n_mem, MEM_HEADS, MEM_DH))
        outs["mv_p"].append(mkv4[0, :, :, MEM_W:].reshape(bp, n_mem, MEM_HEADS, MEM_DH))

        zs = _matmul(xs, w_in_p, g=g_mix[l], name="in_proj_s")
        zs3 = zs[:bs].reshape(bs, 1, N_PACK)
        ff_col = zs[:bs, OFF_SM + SM_FF:OFF_SM + SM_FF + FOX_HEADS].reshape(bs, FOX_HEADS, 1)
        heads = lambda off: zs[:bs, off:off + FOX_W].reshape(bs, FOX_HEADS, FOX_DH)
        o_fox_s, lf_s = _fox_decode(page_table, heads(OFF_FQ), heads(OFF_FK), heads(OFF_FV), ff_col,
                                    b_fox_f[l].reshape(FOX_HEADS, 1), k_pool, v_pool, lf_pool, sf_pool, l)
        o_mem_s = _mem_attn(zs3, mem_k_s, mem_v_s, layer=l, kcol=0, vcol=0, out_dtype=F32)
        o_gla_s, sg_s = _gla_step(zs3, w2p, b_gla_a[l], g_gla[l], state_gla, l)
        o_ret_s, sr_s = _ret_step(zs3, cos_s, sin_s, lg_tab, g_ret[l], state_ret, l)
        xs = _dense_tail(xs, [to_bf16_rows(a) for a in (o_fox_s, o_gla_s, o_ret_s, o_mem_s)], zs, lw)
        outs["fk_s"].append(zs3[:, :, OFF_FK:OFF_FK + FOX_W].reshape(bs, 1, FOX_HEADS, FOX_DH))
        outs["fv_s"].append(zs3[:, :, OFF_FV:OFF_FV + FOX_W].reshape(bs, 1, FOX_HEADS, FOX_DH))
        outs["fl_s"].append(lf_s.reshape(bs, 1, FOX_HEADS))
        outs["sg_s"].append(sg_s)
        outs["sr_s"].append(sr_s)

    y_prompt = _rmsnorm(xp, g_final).reshape(bp, t, d)
    y_sample = _rmsnorm(xs, g_final)[:bs].reshape(bs, 1, d)
    st = lambda k: jnp.stack(outs[k])
    return (y_prompt, y_sample, st("fk_p"), st("fv_p"), st("fl_p"), st("sg_p"), st("sr_p"), st("mk_p"), st("mv_p"),
            st("fk_s"), st("fv_s"), st("fl_s"), st("sg_s"), st("sr_s"))
```

```python
import functools

import numpy as np
import jax
import jax.numpy as jnp
from jax import lax
from jax.experimental import pallas as pl
from jax.experimental.pallas import tpu as pltpu

F32 = jnp.float32
BF16 = jnp.bfloat16

D_MODEL = 2048
FOX_HEADS = 8
FOX_DH = D_MODEL // 16
FOX_W = FOX_HEADS * FOX_DH
GLA_HEADS = 4
GLA_DK = D_MODEL // 16
GLA_DV = D_MODEL // 8
GLA_KW = GLA_HEADS * GLA_DK
GLA_VW = GLA_HEADS * GLA_DV
GLA_RANK = 16
GLA_TAU = 16.0
RET_HEADS = 4
RET_DH = D_MODEL // 8
RET_W = RET_HEADS * RET_DH
MEM_HEADS = 4
MEM_DH = D_MODEL // 8
MEM_W = MEM_HEADS * MEM_DH
N_BRANCH = 4
BRANCH_W = D_MODEL // 2
D_FF = 4 * D_MODEL
PAGE_SIZE = 128
ROPE_BASE = 10000.0
EPS = 1e-6
LOG2E = 1.4426950408889634

LANES = 128
SUBLANES = 8
BF16_ROWS = 16
VMEM_LIMIT_BYTES = 56 * 1024 * 1024

OFF_FQ = 0
OFF_FK = OFF_FQ + FOX_W
OFF_FV = OFF_FK + FOX_W
OFF_GQ = OFF_FV + FOX_W
OFF_GK = OFF_GQ + GLA_KW
OFF_GV = OFF_GK + GLA_KW
OFF_GR = OFF_GV + GLA_VW
OFF_RQ = OFF_GR + GLA_VW
OFF_RK = OFF_RQ + RET_W
OFF_RV = OFF_RK + RET_W
OFF_RG = OFF_RV + RET_W
OFF_MQ = OFF_RG + RET_W
OFF_GT = OFF_MQ + MEM_W
OFF_SM = OFF_GT + N_BRANCH * D_MODEL
SM_FF = 0
SM_GA = FOX_HEADS
N_PACK = OFF_SM + LANES

GLA_CHUNK = 64
GLA_SUB = 8
GLA_SUB_LOG2 = 3
GLA_CHUNKS_PER_STEP = 4
FOX_PAGES_PER_STEP = 8
FOX_DH_LOG2 = 7
assert GLA_SUB == 1 << GLA_SUB_LOG2 and FOX_DH == 1 << FOX_DH_LOG2 and LANES & (LANES - 1) == 0
RET_CHUNK = 256
REPACK_TILE = 1024


def _cparams(sem):
    return pltpu.CompilerParams(dimension_semantics=sem, vmem_limit_bytes=VMEM_LIMIT_BYTES)


def _log_sigmoid(x):
    return jnp.minimum(x, 0.0) - jnp.log1p(jnp.exp(-jnp.abs(x)))


def _sigmoid(x):
    return 1.0 / (1.0 + jnp.exp(-x))


def _dot(a, b):
    return jnp.dot(a, b, preferred_element_type=F32)


def _dot_nt(a, b):
    return lax.dot_general(a, b, (((1,), (1,)), ((), ())), preferred_element_type=F32)


def _dot_tn(a, b):
    return lax.dot_general(a, b, (((0,), (0,)), ((), ())), preferred_element_type=F32)


def _dot01(m01, x):
    hi = x.astype(BF16)
    r1 = x - hi.astype(F32)
    mid = r1.astype(BF16)
    lo = (r1 - mid.astype(F32)).astype(BF16)
    return _dot(m01, hi) + _dot(m01, mid) + _dot(m01, lo)


def _row_to_cols(row):
    return jnp.broadcast_to(row, (LANES, LANES)).T


def _head_norm_gate(o, gain, gate):
    mu = jnp.mean(o, axis=1, keepdims=True)
    d = o - mu
    var = jnp.mean(d * d, axis=1, keepdims=True)
    y = d * lax.rsqrt(var + EPS) * gain
    return y * (gate * _sigmoid(gate))


def _mm_body(*refs, norm, act, has_res, stage, bf16_copy):
    it = iter(refs)
    x_ref = next(it)
    g_ref = next(it) if norm else None
    w_ref = next(it)
    r_ref = next(it) if has_res else None
    o_ref = next(it)
    ob_ref = next(it) if bf16_copy else None
    h_ref = next(it) if stage else None
    if stage:
        @pl.when(pl.program_id(1) == 0)
        def _():
            x = x_ref[...].astype(F32)
            if norm:
                ms = jnp.mean(x * x, axis=-1, keepdims=True)
                x = x * lax.rsqrt(ms + EPS) * g_ref[...]
            h_ref[...] = x.astype(BF16)

        h = h_ref[...]
    else:
        h = x_ref[...]
    acc = _dot(h, w_ref[...])
    if act == "relu2":
        acc = jnp.square(jnp.maximum(acc, 0.0))
    if has_res:
        acc = acc + r_ref[...]
    o_ref[...] = acc.astype(o_ref.dtype)
    if bf16_copy:
        ob_ref[...] = acc.astype(BF16)


def _pick_tile(n, candidates):
    for c in candidates:
        if n % c == 0:
            return c
    return n


def _matmul(x, w, *, g=None, res=None, act=None, out_dtype=F32, bf16_copy=False, name="proj"):
    m, k = x.shape
    n = w.shape[1]
    norm = g is not None
    stage = norm or x.dtype != BF16
    if m >= 1024:
        tm = 1024 if k <= 2048 else 512
    else:
        tm = m
    tn = _pick_tile(n, (1152, 1024, 512))
    if k > 2048:
        tn = min(tn, 512)
    assert m % tm == 0 and n % tn == 0 and tm % BF16_ROWS == 0
    in_specs = [pl.BlockSpec((tm, k), lambda i, j: (i, 0))]
    args = [x]
    if norm:
        in_specs.append(pl.BlockSpec((1, k), lambda i, j: (0, 0)))
        args.append(g.reshape(1, k).astype(F32))
    in_specs.append(pl.BlockSpec((k, tn), lambda i, j: (0, j)))
    args.append(w)
    if res is not None:
        in_specs.append(pl.BlockSpec((tm, tn), lambda i, j: (i, j)))
        args.append(res)
    scratch = [pltpu.VMEM((tm, k), BF16)] if stage else []
    o_spec = pl.BlockSpec((tm, tn), lambda i, j: (i, j))
    o_shape = jax.ShapeDtypeStruct((m, n), out_dtype)
    return pl.pallas_call(
        functools.partial(_mm_body, norm=norm, act=act, has_res=res is not None, stage=stage, bf16_copy=bf16_copy),
        grid=(m // tm, n // tn),
        in_specs=in_specs,
        out_specs=[o_spec, o_spec] if bf16_copy else o_spec,
        out_shape=[o_shape, jax.ShapeDtypeStruct((m, n), BF16)] if bf16_copy else o_shape,
        scratch_shapes=scratch,
        compiler_params=_cparams(("parallel", "arbitrary")),
        name=name,
    )(*args)


def _rmsnorm_body(x_ref, g_ref, o_ref):
    x = x_ref[...]
    ms = jnp.mean(x * x, axis=-1, keepdims=True)
    o_ref[...] = x * lax.rsqrt(ms + EPS) * g_ref[...]


def _rmsnorm(x, g):
    m, k = x.shape
    tm = _pick_tile(m, (512,))
    return pl.pallas_call(
        _rmsnorm_body,
        grid=(m // tm,),
        in_specs=[pl.BlockSpec((tm, k), lambda i: (i, 0)), pl.BlockSpec((1, k), lambda i: (0, 0))],
        out_specs=pl.BlockSpec((tm, k), lambda i: (i, 0)),
        out_shape=jax.ShapeDtypeStruct((m, k), F32),
        compiler_params=_cparams(("parallel",)),
        name="final_norm",
    )(x, g.reshape(1, k))


def _fox_gate_body(x_ref, b_ref, lf_ref, c_ref):
    lf = _log_sigmoid(x_ref[...] + b_ref[...])
    lf_ref[...] = lf
    t = lf.shape[1]
    lane = lax.broadcasted_iota(jnp.int32, lf.shape, 1)
    c = lf
    shift = 1
    while shift < t:
        c = c + jnp.where(lane >= shift, pltpu.roll(c, shift, axis=1), 0.0)
        shift *= 2
    c_ref[...] = c


def _fox_gate(ff_t, bias):
    b, h, t = ff_t.shape
    spec = pl.BlockSpec((None, h, t), lambda i: (i, 0, 0))
    return pl.pallas_call(
        _fox_gate_body,
        grid=(b,),
        in_specs=[spec, pl.BlockSpec((h, 1), lambda i: (0, 0))],
        out_specs=[spec, spec],
        out_shape=[jax.ShapeDtypeStruct((b, h, t), F32)] * 2,
        compiler_params=_cparams(("parallel",)),
        name="fox_gate",
    )(ff_t, bias.reshape(h, 1))


def _fox_attn_body(q_ref, k_ref, v_ref, ck_ref, o_ref, q_sc, m_sc, l_sc, acc_sc, *, tile):
    qi = pl.program_id(1)
    ki = pl.program_id(2)
    nk = pl.num_programs(2)

    @pl.when(ki == 0)
    def _():
        q_sc[...] = (q_ref[...] * (FOX_DH ** -0.5 * LOG2E)).astype(BF16)
        m_sc[...] = jnp.full(m_sc.shape, -jnp.inf, F32)
        l_sc[...] = jnp.zeros(l_sc.shape, F32)
        acc_sc[...] = jnp.zeros(acc_sc.shape, F32)

    def step(on_diagonal):
        ck = ck_ref[...] * LOG2E
        if on_diagonal:
            row = lax.broadcasted_iota(jnp.int32, (tile, tile), 0)
            col = lax.broadcasted_iota(jnp.int32, (tile, tile), 1)
            visible = col <= row
        for h in range(FOX_HEADS):
            sl = slice(h * FOX_DH, (h + 1) * FOX_DH)
            s = _dot_nt(q_sc[:, sl], k_ref[:, sl]) - ck[h:h + 1, :]
            if on_diagonal:
                s = jnp.where(visible, s, -jnp.inf)
            m_prev = m_sc[h]
            m_new = jnp.maximum(m_prev, jnp.max(s, axis=1, keepdims=True))
            alpha = jnp.exp2(m_prev - m_new)
            p = jnp.exp2(s - m_new[:, :1])
            l_sc[h] = alpha * l_sc[h] + jnp.sum(p, axis=1, keepdims=True)
            acc_sc[:, sl] = alpha * acc_sc[:, sl] + _dot(p.astype(BF16), v_ref[:, sl])
            m_sc[h] = m_new

    pl.when(ki < qi)(functools.partial(step, False))
    pl.when(ki == qi)(functools.partial(step, True))

    @pl.when(ki == nk - 1)
    def _():
        for h in range(FOX_HEADS):
            sl = slice(h * FOX_DH, (h + 1) * FOX_DH)
            o_ref[:, sl] = (acc_sc[:, sl] / l_sc[h]).astype(o_ref.dtype)


def _fox_attn(z3, zb3, c_row):
    b, t, _ = z3.shape
    tile = _pick_tile(t, (512, 256, 128))
    n = t // tile
    qcol, kcol, vcol = OFF_FQ // FOX_W, OFF_FK // FOX_W, OFF_FV // FOX_W
    return pl.pallas_call(
        functools.partial(_fox_attn_body, tile=tile),
        grid=(b, n, n),
        in_specs=[
            pl.BlockSpec((None, tile, FOX_W), lambda bi, qi, ki: (bi, qi, qcol)),
            pl.BlockSpec((None, tile, FOX_W), lambda bi, qi, ki: (bi, jnp.minimum(ki, qi), kcol)),
            pl.BlockSpec((None, tile, FOX_W), lambda bi, qi, ki: (bi, jnp.minimum(ki, qi), vcol)),
            pl.BlockSpec((None, FOX_HEADS, tile), lambda bi, qi, ki: (bi, 0, jnp.minimum(ki, qi))),
        ],
        out_specs=pl.BlockSpec((None, tile, FOX_W), lambda bi, qi, ki: (bi, qi, 0)),
        out_shape=jax.ShapeDtypeStruct((b, t, FOX_W), BF16),
        scratch_shapes=[
            pltpu.VMEM((tile, FOX_W), BF16),
            pltpu.VMEM((FOX_HEADS, tile, FOX_DH), F32),
            pltpu.VMEM((FOX_HEADS, tile, FOX_DH), F32),
            pltpu.VMEM((tile, FOX_W), F32),
        ],
        compiler_params=_cparams(("parallel", "parallel", "arbitrary")),
        name="fox_attn",
    )(z3, zb3, zb3, c_row)


def _mem_attn_body(q_ref, k_ref, v_ref, o_ref, *, rows):
    scale = MEM_DH ** -0.5
    for h in range(MEM_HEADS):
        sl = slice(h * MEM_DH, (h + 1) * MEM_DH)
        q = q_ref[:, sl]
        if rows < SUBLANES:
            q = jnp.broadcast_to(q, (SUBLANES, MEM_DH))
        s = _dot_nt(q.astype(BF16), k_ref[:, sl].astype(BF16)) * scale
        m = jnp.max(s, axis=1, keepdims=True)
        p = jnp.exp(s - m)
        p = p / jnp.sum(p, axis=1, keepdims=True)
        o = _dot(p.astype(BF16), v_ref[:, sl].astype(BF16))
        o_ref[:, sl] = o[:rows].astype(o_ref.dtype)


def _mem_attn(z3, k4, v4, *, layer, kcol, vcol, out_dtype):
    b, t, _ = z3.shape
    n_mem = k4.shape[2]
    tq = _pick_tile(t, (512, 256, 128))
    qcol = OFF_MQ // MEM_W
    return pl.pallas_call(
        functools.partial(_mem_attn_body, rows=tq),
        grid=(b, t // tq),
        in_specs=[
            pl.BlockSpec((None, tq, MEM_W), lambda bi, qi: (bi, qi, qcol)),
            pl.BlockSpec((None, None, n_mem, MEM_W), lambda bi, qi: (layer, bi, 0, kcol)),
            pl.BlockSpec((None, None, n_mem, MEM_W), lambda bi, qi: (layer, bi, 0, vcol)),
        ],
        out_specs=pl.BlockSpec((None, tq, MEM_W), lambda bi, qi: (bi, qi, 0)),
        out_shape=jax.ShapeDtypeStruct((b, t, MEM_W), out_dtype),
        compiler_params=_cparams(("parallel", "parallel")),
        name="mem_attn",
    )(z3, k4, v4)


def _rotary(x, cos, sin):
    half = x.shape[1] // 2
    x1, x2 = x[:, :half], x[:, half:]
    return jnp.concatenate([x1 * cos - x2 * sin, x1 * sin + x2 * cos], axis=1)


def _ret_body(q_ref, k_ref, v_ref, g_ref, cos_ref, sin_ref, lg_ref, gn_ref, o_ref, s_ref, st_sc, *, chunk):
    c = pl.program_id(1)
    nc = pl.num_programs(1)

    @pl.when(c == 0)
    def _():
        st_sc[...] = jnp.zeros(st_sc.shape, F32)

    cos, sin = cos_ref[...], sin_ref[...]
    row = lax.broadcasted_iota(jnp.int32, (chunk, chunk), 0)
    col = lax.broadcasted_iota(jnp.int32, (chunk, chunk), 1)
    lag = (row - col).astype(F32)
    t = lax.broadcasted_iota(jnp.int32, (chunk, 1), 0).astype(F32)
    for h in range(RET_HEADS):
        sl = slice(h * RET_DH, (h + 1) * RET_DH)
        lg = lg_ref[h][:, :1]
        q = _rotary(q_ref[:, sl], cos, sin)
        k = _rotary(k_ref[:, sl], cos, sin) * (RET_DH ** -0.5)
        v = v_ref[:, sl].astype(BF16)
        decay = jnp.where(col <= row, jnp.exp(lag * lg), 0.0)
        a = _dot_nt(q.astype(BF16), k.astype(BF16)) * decay
        s0 = st_sc[h]
        o = _dot(a.astype(BF16), v) + jnp.exp((t + 1.0) * lg) * _dot(q.astype(BF16), s0.astype(BF16))
        k_end = k * jnp.exp((chunk - 1.0 - t) * lg)
        s_new = jnp.exp(chunk * lg) * s0 + _dot_tn(k_end.astype(BF16), v)
        st_sc[h] = s_new
        o_ref[:, sl] = _head_norm_gate(o, gn_ref[:, sl], g_ref[:, sl]).astype(o_ref.dtype)

        @pl.when(c == nc - 1)
        def _():
            s_ref[h] = s_new


def _ret_prompt(z3, cos, sin, lg_tab, g_ret):
    b, t, _ = z3.shape
    chunk = _pick_tile(t, (RET_CHUNK, 128, 64))
    zspec = lambda off: pl.BlockSpec((None, chunk, RET_W), lambda bi, c: (bi, c, off // RET_W))
    return pl.pallas_call(
        functools.partial(_ret_body, chunk=chunk),
        grid=(b, t // chunk),
        in_specs=[
            zspec(OFF_RQ), zspec(OFF_RK), zspec(OFF_RV), zspec(OFF_RG),
            pl.BlockSpec((chunk, RET_DH // 2), lambda bi, c: (c, 0)),
            pl.BlockSpec((chunk, RET_DH // 2), lambda bi, c: (c, 0)),
            pl.BlockSpec((RET_HEADS, 1, LANES), lambda bi, c: (0, 0, 0)),
            pl.BlockSpec((1, RET_W), lambda bi, c: (0, 0)),
        ],
        out_specs=[
            pl.BlockSpec((None, chunk, RET_W), lambda bi, c: (bi, c, 0)),
            pl.BlockSpec((None, RET_HEADS, RET_DH, RET_DH), lambda bi, c: (bi, 0, 0, 0)),
        ],
        out_shape=[
            jax.ShapeDtypeStruct((b, t, RET_W), BF16),
            jax.ShapeDtypeStruct((b, RET_HEADS, RET_DH, RET_DH), F32),
        ],
        scratch_shapes=[pltpu.VMEM((RET_HEADS, RET_DH, RET_DH), F32)],
        compiler_params=_cparams(("parallel", "arbitrary")),
        name="ret_prompt",
    )(z3, z3, z3, z3, cos, sin, lg_tab, g_ret.reshape(1, RET_W))


def _gla_loga(zs, w2_ref, ba_ref):
    pre = _dot(zs.astype(BF16), w2_ref[...]) + ba_ref[...]
    return _log_sigmoid(pre) * (1.0 / GLA_TAU)


def _gla_chunk(q, k, v_ref, gr_ref, gn_ref, o_ref, loga, st, tri, hsum, rows):
    chunk = q.shape[0]
    n_sub = chunk // GLA_SUB
    b = _dot01(tri, loga)
    row = lax.broadcasted_iota(jnp.int32, (chunk, GLA_KW), 0)
    b_last = b[chunk - 1:chunk, :]
    beta = jnp.zeros_like(b)
    for i in range(1, n_sub):
        beta = jnp.where(row >= i * GLA_SUB, b[i * GLA_SUB - 1:i * GLA_SUB, :], beta)
    q_rel = q * jnp.exp(b - beta)
    q_abs = q * jnp.exp(b)
    k_end = k * jnp.exp(b_last - b)
    k_rel = []
    for i in range(1, n_sub):
        bi = b[i * GLA_SUB - 1:i * GLA_SUB, :]
        k_rel.append(jnp.where(row < i * GLA_SUB, k * jnp.exp(jnp.minimum(bi - b, 0.0)), 0.0).astype(BF16))

    prods = []
    for delta in range(GLA_SUB):
        kd = k if delta == 0 else pltpu.roll(k, delta, axis=0)
        bd = b if delta == 0 else pltpu.roll(b, delta, axis=0)
        prods.append((q * kd * jnp.exp(jnp.minimum(b - bd, 0.0))).astype(BF16))
    sums = _dot(jnp.concatenate(prods, axis=0), hsum)
    rowb = lax.broadcasted_iota(jnp.int32, (chunk, GLA_HEADS * LANES), 0)
    colb = lax.broadcasted_iota(jnp.int32, (chunk, GLA_HEADS * LANES), 1) & (LANES - 1)
    rsub = rowb & (GLA_SUB - 1)
    band = jnp.zeros((chunk, GLA_HEADS * LANES), F32)
    for delta in range(GLA_SUB):
        keep = jnp.logical_and(colb == rowb - delta, rsub >= delta)
        band = band + jnp.where(keep, sums[delta * chunk:(delta + 1) * chunk], 0.0)

    sub_of_row = row[:, :GLA_DK] >> GLA_SUB_LOG2
    new_st = []
    for h in range(GLA_HEADS):
        ks = slice(h * GLA_DK, (h + 1) * GLA_DK)
        vs = slice(h * GLA_DV, (h + 1) * GLA_DV)
        v = v_ref[rows, vs].astype(BF16)
        lhs = jnp.concatenate(
            [jnp.where(sub_of_row == i, q_rel[:, ks], 0.0) for i in range(1, n_sub)], axis=1).astype(BF16)
        rhs = jnp.concatenate([kr[:, ks] for kr in k_rel], axis=1)
        a = _dot_nt(lhs, rhs) + band[:, h * LANES:h * LANES + chunk]
        o = _dot(a.astype(BF16), v) + _dot_nt(q_abs[:, ks].astype(BF16), st[h].astype(BF16))
        new_st.append(st[h] * jnp.exp(b_last[:, ks]) + _dot_tn(v, k_end[:, ks].astype(BF16)))
        o_ref[rows, vs] = _head_norm_gate(o, gn_ref[:, vs], gr_ref[rows, vs]).astype(o_ref.dtype)
    return new_st


def _gla_body(q_ref, k_ref, v_ref, gr_ref, zs_ref, w2_ref, ba_ref, gn_ref, tri_ref, hsum_ref,
              o_ref, s_ref, st_sc, *, chunk, n_chunks):
    c = pl.program_id(1)
    nc = pl.num_programs(1)

    @pl.when(c == 0)
    def _():
        st_sc[...] = jnp.zeros(st_sc.shape, F32)

    loga = _gla_loga(zs_ref[...], w2_ref, ba_ref)
    tri, hsum = tri_ref[...], hsum_ref[...]
    st = [st_sc[h] for h in range(GLA_HEADS)]
    for ci in range(n_chunks):
        rows = slice(ci * chunk, (ci + 1) * chunk)
        q = q_ref[rows, :] * (GLA_DK ** -0.5)
        st = _gla_chunk(q, k_ref[rows, :], v_ref, gr_ref, gn_ref, o_ref, loga[rows], st, tri, hsum, rows)
    for h in range(GLA_HEADS):
        st_sc[h] = st[h]

    @pl.when(c == nc - 1)
    def _():
        for h in range(GLA_HEADS):
            s_ref[h] = st[h].T


def _gla_consts(chunk):
    r = np.arange(chunk)
    tri = (r[None, :] <= r[:, None]).astype(np.float32)
    lane = np.arange(GLA_KW)
    out = np.arange(GLA_HEADS * LANES)
    hsum = (lane[:, None] // GLA_DK == out[None, :] // LANES).astype(np.float32)
    return jnp.asarray(tri, BF16), jnp.asarray(hsum, BF16)


def _gla_prompt(z3, w2p, b_a, g_gla):
    b, t, _ = z3.shape
    chunk = GLA_CHUNK
    n_chunks = _pick_tile(t // chunk, (GLA_CHUNKS_PER_STEP, 2, 1))
    rows = chunk * n_chunks
    assert t % rows == 0
    tri, hsum = _gla_consts(chunk)
    full = lambda shape: pl.BlockSpec(shape, lambda bi, c: (0,) * len(shape))
    return pl.pallas_call(
        functools.partial(_gla_body, chunk=chunk, n_chunks=n_chunks),
        grid=(b, t // rows),
        in_specs=[
            pl.BlockSpec((None, rows, GLA_KW), lambda bi, c: (bi, c, OFF_GQ // GLA_KW)),
            pl.BlockSpec((None, rows, GLA_KW), lambda bi, c: (bi, c, OFF_GK // GLA_KW)),
            pl.BlockSpec((None, rows, GLA_VW), lambda bi, c: (bi, c, OFF_GV // GLA_VW)),
            pl.BlockSpec((None, rows, GLA_VW), lambda bi, c: (bi, c, OFF_GR // GLA_VW)),
            pl.BlockSpec((None, rows, LANES), lambda bi, c: (bi, c, OFF_SM // LANES)),
            full((LANES, GLA_KW)), full((1, GLA_KW)), full((1, GLA_VW)),
            full((chunk, chunk)), full((GLA_KW, GLA_HEADS * LANES)),
        ],
        out_specs=[
            pl.BlockSpec((None, rows, GLA_VW), lambda bi, c: (bi, c, 0)),
            pl.BlockSpec((None, GLA_HEADS, GLA_DK, GLA_DV), lambda bi, c: (bi, 0, 0, 0)),
        ],
        out_shape=[
            jax.ShapeDtypeStruct((b, t, GLA_VW), BF16),
            jax.ShapeDtypeStruct((b, GLA_HEADS, GLA_DK, GLA_DV), F32),
        ],
        scratch_shapes=[pltpu.VMEM((GLA_HEADS, GLA_DV, GLA_DK), F32)],
        compiler_params=_cparams(("parallel", "arbitrary")),
        name="gla_prompt",
    )(z3, z3, z3, z3, z3, w2p, b_a.reshape(1, GLA_KW), g_gla.reshape(1, GLA_VW), tri, hsum)


def _gla_step_body(q_ref, k_ref, v_ref, gr_ref, zs_ref, w2_ref, ba_ref, gn_ref, s0_ref, o_ref, s_ref):
    zs = jnp.broadcast_to(zs_ref[...], (BF16_ROWS, LANES))
    loga = _gla_loga(zs, w2_ref, ba_ref)[:1]
    q = q_ref[...] * (GLA_DK ** -0.5)
    k = k_ref[...]
    for h in range(GLA_HEADS):
        ks = slice(h * GLA_DK, (h + 1) * GLA_DK)
        vs = slice(h * GLA_DV, (h + 1) * GLA_DV)
        widen = lambda m: jnp.concatenate([m] * (GLA_DV // LANES), axis=1)
        a_c = widen(_row_to_cols(jnp.exp(loga[:, ks])))
        k_c = widen(_row_to_cols(k[:, ks]))
        q_c = widen(_row_to_cols(q[:, ks]))
        s_new = a_c * s0_ref[h] + k_c * v_ref[:, vs]
        s_ref[h] = s_new
        o = jnp.sum(q_c * s_new, axis=0, keepdims=True)
        o_ref[:, vs] = _head_norm_gate(o, gn_ref[:, vs], gr_ref[:, vs])


def _gla_step(zs3, w2p, b_a, g_gla, state, layer):
    b = zs3.shape[0]
    full = lambda shape: pl.BlockSpec(shape, lambda bi: (0,) * len(shape))
    return pl.pallas_call(
        _gla_step_body,
        grid=(b,),
        in_specs=[
            pl.BlockSpec((None, 1, GLA_KW), lambda bi: (bi, 0, OFF_GQ // GLA_KW)),
            pl.BlockSpec((None, 1, GLA_KW), lambda bi: (bi, 0, OFF_GK // GLA_KW)),
            pl.BlockSpec((None, 1, GLA_VW), lambda bi: (bi, 0, OFF_GV // GLA_VW)),
            pl.BlockSpec((None, 1, GLA_VW), lambda bi: (bi, 0, OFF_GR // GLA_VW)),
            pl.BlockSpec((None, 1, LANES), lambda bi: (bi, 0, OFF_SM // LANES)),
            full((LANES, GLA_KW)), full((1, GLA_KW)), full((1, GLA_VW)),
            pl.BlockSpec((None, None, GLA_HEADS, GLA_DK, GLA_DV), lambda bi: (layer, bi, 0, 0, 0)),
        ],
        out_specs=[
            pl.BlockSpec((None, 1, GLA_VW), lambda bi: (bi, 0, 0)),
            pl.BlockSpec((None, GLA_HEADS, GLA_DK, GLA_DV), lambda bi: (bi, 0, 0, 0)),
        ],
        out_shape=[
            jax.ShapeDtypeStruct((b, 1, GLA_VW), F32),
            jax.ShapeDtypeStruct((b, GLA_HEADS, GLA_DK, GLA_DV), F32),
        ],
        compiler_params=_cparams(("parallel",)),
        name="gla_step",
    )(zs3, zs3, zs3, zs3, zs3, w2p, b_a.reshape(1, GLA_KW), g_gla.reshape(1, GLA_VW), state)


def _ret_step_body(q_ref, k_ref, v_ref, g_ref, cos_ref, sin_ref, lg_ref, gn_ref, s0_ref, o_ref, s_ref):
    cos, sin = cos_ref[...], sin_ref[...]
    half = RET_DH // 2
    for h in range(RET_HEADS):
        sl = slice(h * RET_DH, (h + 1) * RET_DH)
        q = _rotary(q_ref[:, sl], cos, sin)
        k = _rotary(k_ref[:, sl], cos, sin) * (RET_DH ** -0.5)
        cols = lambda r: jnp.concatenate(
            [jnp.concatenate([_row_to_cols(r[:, i * half:(i + 1) * half])] * 2, axis=1) for i in range(2)], axis=0)
        gamma = jnp.exp(lg_ref[h][:, :1])
        s_new = gamma * s0_ref[h] + cols(k) * v_ref[:, sl]
        s_ref[h] = s_new
        o = jnp.sum(cols(q) * s_new, axis=0, keepdims=True)
        o_ref[:, sl] = _head_norm_gate(o, gn_ref[:, sl], g_ref[:, sl])


def _ret_step(zs3, cos, sin, lg_tab, g_ret, state, layer):
    b = zs3.shape[0]
    full = lambda shape: pl.BlockSpec(shape, lambda bi: (0,) * len(shape))
    zspec = lambda off: pl.BlockSpec((None, 1, RET_W), lambda bi: (bi, 0, off // RET_W))
    return pl.pallas_call(
        _ret_step_body,
        grid=(b,),
        in_specs=[
            zspec(OFF_RQ), zspec(OFF_RK), zspec(OFF_RV), zspec(OFF_RG),
            full((1, RET_DH // 2)), full((1, RET_DH // 2)), full((RET_HEADS, 1, LANES)), full((1, RET_W)),
            pl.BlockSpec((None, None, RET_HEADS, RET_DH, RET_DH), lambda bi: (layer, bi, 0, 0, 0)),
        ],
        out_specs=[
            pl.BlockSpec((None, 1, RET_W), lambda bi: (bi, 0, 0)),
            pl.BlockSpec((None, RET_HEADS, RET_DH, RET_DH), lambda bi: (bi, 0, 0, 0)),
        ],
        out_shape=[
            jax.ShapeDtypeStruct((b, 1, RET_W), F32),
            jax.ShapeDtypeStruct((b, RET_HEADS, RET_DH, RET_DH), F32),
        ],
        compiler_params=_cparams(("parallel",)),
        name="ret_step",
    )(zs3, zs3, zs3, zs3, cos, sin, lg_tab, g_ret.reshape(1, RET_W), state)


def _page_suffix_body(x_ref, o_ref):
    x = x_ref[...]
    width = x.shape[1]
    lane = lax.broadcasted_iota(jnp.int32, x.shape, 1)
    incl = x
    shift = FOX_HEADS
    while shift < width:
        moved = pltpu.roll(incl, width - shift, axis=1)
        incl = incl + jnp.where(lane < width - shift, moved, 0.0)
        shift *= 2
    o_ref[...] = incl


def _page_suffix(lf_pages):
    n, width = lf_pages.shape
    rows = _pick_tile(n, (64, 32, 16, 8))
    spec = pl.BlockSpec((rows, width), lambda i: (i, 0))
    return pl.pallas_call(
        _page_suffix_body,
        grid=(n // rows,),
        in_specs=[spec],
        out_specs=spec,
        out_shape=jax.ShapeDtypeStruct((n, width), F32),
        compiler_params=_cparams(("parallel",)),
        name="page_suffix",
    )(lf_pages)


def _fox_decode_body(*refs, pages):
    q_ref, kn_ref, vn_ref, ff_ref, bf_ref = refs[1:6]
    kp_refs = refs[6:6 + pages]
    vp_refs = refs[6 + pages:6 + 2 * pages]
    lp_refs = refs[6 + 2 * pages:6 + 3 * pages]
    sf_refs = refs[6 + 3 * pages:6 + 4 * pages]
    o_ref, lf_ref, m_sc, l_sc, tail_sc, acc_sc = refs[6 + 4 * pages:]
    g = pl.program_id(1)
    n_groups = pl.num_programs(1)
    scale = FOX_DH ** -0.5
    width = PAGE_SIZE * FOX_HEADS
    sub = lax.broadcasted_iota(jnp.int32, (FOX_HEADS, width), 0)
    lane = lax.broadcasted_iota(jnp.int32, (FOX_HEADS, width), 1)
    own = (lane & (FOX_HEADS - 1)) == sub
    q = q_ref[...]
    qb = q.astype(BF16)

    @pl.when(g == 0)
    def _():
        lf_new = _log_sigmoid(ff_ref[...] + bf_ref[...])
        lf_ref[...] = lf_new
        tail_sc[...] = jnp.broadcast_to(lf_new, tail_sc.shape)
        s_new = jnp.sum(q * kn_ref[...], axis=1, keepdims=True) * scale
        m_sc[...] = jnp.broadcast_to(s_new, m_sc.shape)
        l_sc[...] = jnp.ones(l_sc.shape, F32)
        acc_sc[...] = vn_ref[...]

    tail = tail_sc[:, :1]
    scores = []
    for i in range(pages):
        incl = jnp.broadcast_to(sf_refs[i][...], (FOX_HEADS, width))
        bias = tail + (incl - lp_refs[i][...])
        s = _dot_nt(qb, kp_refs[i][...].astype(BF16)) * scale + bias
        scores.append(jnp.where(own, s, -jnp.inf))
        tail = tail + jnp.sum(jnp.where(lane == sub, incl, 0.0), axis=1, keepdims=True)
    m_prev = m_sc[:, :1]
    m_new = m_prev
    for s in scores:
        m_new = jnp.maximum(m_new, jnp.max(s, axis=1, keepdims=True))
    alpha = jnp.exp(m_prev - m_new)
    l_new = alpha * l_sc[:, :1]
    acc = alpha * acc_sc[...]
    for i in range(pages):
        pr = jnp.exp(scores[i] - m_new)
        l_new = l_new + jnp.sum(pr, axis=1, keepdims=True)
        acc = acc + _dot(pr.astype(BF16), vp_refs[i][...].astype(BF16))
    acc_sc[...] = acc
    l_sc[...] = jnp.broadcast_to(l_new, l_sc.shape)
    m_sc[...] = jnp.broadcast_to(m_new, m_sc.shape)
    tail_sc[...] = jnp.broadcast_to(tail, tail_sc.shape)

    @pl.when(g == n_groups - 1)
    def _():
        o_ref[...] = acc / l_new


def _fox_decode(page_table, q8, kn8, vn8, ff_col, bf_col, k_pool, v_pool, lf_pool, sf_pool, layer):
    b, n_pages = page_table.shape
    width = PAGE_SIZE * FOX_HEADS
    pages = _pick_tile(n_pages, (FOX_PAGES_PER_STEP, 2, 1))

    def page(i):
        return lambda bi, g, pt: (layer, pt[bi, n_pages - 1 - (g * pages + i)], 0, 0)

    tok = pl.BlockSpec((None, FOX_HEADS, FOX_DH), lambda bi, g, pt: (bi, 0, 0))
    col = pl.BlockSpec((None, FOX_HEADS, 1), lambda bi, g, pt: (bi, 0, 0))
    kv_specs = [pl.BlockSpec((None, None, width, FOX_DH), page(i)) for i in range(pages)]
    row_specs = [pl.BlockSpec((None, None, 1, width), page(i)) for i in range(pages)]
    grid_spec = pltpu.PrefetchScalarGridSpec(
        num_scalar_prefetch=1,
        grid=(b, n_pages // pages),
        in_specs=[tok, tok, tok, col, pl.BlockSpec((FOX_HEADS, 1), lambda bi, g, pt: (0, 0))]
        + kv_specs + kv_specs + row_specs + row_specs,
        out_specs=[tok, col],
        scratch_shapes=[
            pltpu.VMEM((FOX_HEADS, LANES), F32),
            pltpu.VMEM((FOX_HEADS, LANES), F32),
            pltpu.VMEM((FOX_HEADS, LANES), F32),
            pltpu.VMEM((FOX_HEADS, FOX_DH), F32),
        ],
    )
    return pl.pallas_call(
        functools.partial(_fox_decode_body, pages=pages),
        grid_spec=grid_spec,
        out_shape=[
            jax.ShapeDtypeStruct((b, FOX_HEADS, FOX_DH), F32),
            jax.ShapeDtypeStruct((b, FOX_HEADS, 1), F32),
        ],
        compiler_params=_cparams(("parallel", "arbitrary")),
        name="fox_decode",
    )(page_table, q8, kn8, vn8, ff_col, bf_col, *([k_pool] * pages), *([v_pool] * pages),
      *([lf_pool] * pages), *([sf_pool] * pages))


def _merge_body(o0, o1, o2, o3, g0, g1, g2, g3, w_ref, out_ref):
    acc = None
    for i, (o_ref, g_ref) in enumerate(((o0, g0), (o1, g1), (o2, g2), (o3, g3))):
        y = (0.5 * jnp.tanh(0.5 * g_ref[...]) + 0.5) * _dot(o_ref[...], w_ref[i])
        acc = y if acc is None else acc + y
    out_ref[...] = acc.astype(out_ref.dtype)


def _merge(branches, z, w_branch):
    m = z.shape[0]
    tm = _pick_tile(m, (512,))
    tn = 512
    assert D_MODEL % tn == 0 and OFF_GT % tn == 0
    o_spec = pl.BlockSpec((tm, BRANCH_W), lambda i, j: (i, 0))
    g_spec = lambda br: pl.BlockSpec((tm, tn), lambda i, j: (i, (OFF_GT + br * D_MODEL) // tn + j))
    return pl.pallas_call(
        _merge_body,
        grid=(m // tm, D_MODEL // tn),
        in_specs=[o_spec] * N_BRANCH + [g_spec(br) for br in range(N_BRANCH)]
        + [pl.BlockSpec((N_BRANCH, BRANCH_W, tn), lambda i, j: (0, 0, j))],
        out_specs=pl.BlockSpec((tm, tn), lambda i, j: (i, j)),
        out_shape=jax.ShapeDtypeStruct((m, D_MODEL), BF16),
        compiler_params=_cparams(("parallel", "arbitrary")),
        name="merge",
    )(*branches, z, z, z, z, w_branch)


def _heads_out_body(k_ref, v_ref, k_any, v_any, ko_ref, vo_ref):
    del k_any, v_any
    for h in range(FOX_HEADS):
        sl = slice(h * FOX_DH, (h + 1) * FOX_DH)
        ko_ref[:, h, :] = k_ref[:, sl]
        vo_ref[:, h, :] = v_ref[:, sl]


def _heads_out(z, k_stack, v_stack, layer):
    m = z.shape[0]
    tm = _pick_tile(m, (512,))
    o_spec = pl.BlockSpec((None, tm, FOX_HEADS, FOX_DH), lambda i: (layer, i, 0, 0))
    return pl.pallas_call(
        _heads_out_body,
        grid=(m // tm,),
        in_specs=[
            pl.BlockSpec((tm, FOX_W), lambda i: (i, OFF_FK // FOX_W)),
            pl.BlockSpec((tm, FOX_W), lambda i: (i, OFF_FV // FOX_W)),
            pl.BlockSpec(memory_space=pl.ANY),
            pl.BlockSpec(memory_space=pl.ANY),
        ],
        out_specs=[o_spec, o_spec],
        out_shape=[jax.ShapeDtypeStruct(k_stack.shape, F32), jax.ShapeDtypeStruct(v_stack.shape, F32)],
        input_output_aliases={2: 0, 3: 1},
        compiler_params=_cparams(("parallel",)),
        name="heads_out",
    )(z, z, k_stack, v_stack)


def _repack_body(main_ref, extra_ref, ff_ref, ga_ref, o_ref, *, n_big):
    jt = pl.program_id(1)

    @pl.when(jt < n_big)
    def _():
        start = jt * REPACK_TILE
        shift = jnp.where(start < OFF_GQ, 0, jnp.where(start < OFF_RQ, FOX_HEADS, FOX_HEADS + GLA_RANK))
        x = jnp.concatenate([main_ref[...], extra_ref[...]], axis=1)
        width = x.shape[1]
        y = pltpu.roll(x, (width - shift) % width, axis=1)
        o_ref[...] = y[:, :REPACK_TILE].astype(BF16)

    @pl.when(jt == n_big)
    def _():
        lane = lax.broadcasted_iota(jnp.int32, ff_ref.shape, 1)
        small = jnp.where(lane < SM_GA, ff_ref[...], jnp.where(lane < SM_GA + GLA_RANK, ga_ref[...], 0.0))
        rest = jnp.zeros((small.shape[0], REPACK_TILE - LANES), F32)
        o_ref[...] = jnp.concatenate([small, rest], axis=1).astype(BF16)


def _pack_w_in(w_in, layer):
    _, k, d_in = w_in.shape
    tk = _pick_tile(k, (512,))
    c_ff = 3 * FOX_W
    c_ga = c_ff + FOX_HEADS + 2 * GLA_KW + 2 * GLA_VW
    n_big = OFF_SM // REPACK_TILE
    assert OFF_GQ % REPACK_TILE == 0 and OFF_RQ % REPACK_TILE == 0 and OFF_SM % REPACK_TILE == 0
    assert c_ff % LANES == SM_FF and c_ga % LANES == SM_GA and SM_GA == SM_FF + FOX_HEADS
    last_lane_block = (d_in - 1) // LANES
    per_tile = REPACK_TILE // LANES
    return pl.pallas_call(
        functools.partial(_repack_body, n_big=n_big),
        grid=(k // tk, n_big + 1),
        in_specs=[
            pl.BlockSpec((None, tk, REPACK_TILE), lambda i, j: (layer, i, jnp.minimum(j, n_big - 1))),
            pl.BlockSpec((None, tk, LANES), lambda i, j: (layer, i, jnp.minimum((j + 1) * per_tile, last_lane_block))),
            pl.BlockSpec((None, tk, LANES), lambda i, j: (layer, i, c_ff // LANES)),
            pl.BlockSpec((None, tk, LANES), lambda i, j: (layer, i, c_ga // LANES)),
        ],
        out_specs=pl.BlockSpec((tk, REPACK_TILE), lambda i, j: (i, j)),
        out_shape=jax.ShapeDtypeStruct((k, N_PACK), BF16),
        compiler_params=_cparams(("parallel", "arbitrary")),
        name="repack_w_in",
    )(w_in, w_in, w_in, w_in)


def _pack_w_gla_a2(w2):
    out = jnp.zeros((LANES, GLA_KW), F32).at[SM_GA:SM_GA + GLA_RANK].set(w2)
    return out.astype(BF16)


def _rope_tables(pos):
    half = RET_DH // 2
    inv = ROPE_BASE ** (-jnp.arange(half, dtype=F32) / half)
    ang = pos.astype(F32)[:, None] * inv[None, :]
    return jnp.cos(ang), jnp.sin(ang)


def _retention_log_decay_table():
    lg = jnp.log1p(-jnp.exp2(-5.0 - jnp.arange(RET_HEADS, dtype=F32)))
    return jnp.broadcast_to(lg[:, None, None], (RET_HEADS, 1, LANES))


def _dense_tail(x, branches, z, lw):
    merged = _merge(branches, z, lw["w_branch"])
    x = _matmul(merged, lw["w_out"], res=x, name="out_proj")
    u = _matmul(x, lw["w_ff1"], g=lw["g_mlp"], act="relu2", out_dtype=BF16, name="ff1")
    return _matmul(u, lw["w_ff2"], res=x, name="ff2")


def kernel(x_prompt, x_sample, cache_fox_k, cache_fox_v, cache_fox_logf, state_gla, state_ret, cache_mem_k, cache_mem_v, page_table, mem_prompt, g_mix, w_in, b_fox_f, w_gla_a2, b_gla_a, g_gla, g_ret, w_mem_kv, w_branch, w_out, g_mlp, w_ff1, w_ff2, g_final):
    depth = w_in.shape[0]
    bp, t, d = x_prompt.shape
    bs, ts, _ = x_sample.shape
    assert ts == 1 and d == D_MODEL
    n_pool = cache_fox_k.shape[1]
    n_mem = mem_prompt.shape[1]
    past = page_table.shape[1] * PAGE_SIZE
    rows_s = -(-bs // BF16_ROWS) * BF16_ROWS

    cos_p, sin_p = _rope_tables(jnp.arange(t))
    cos_s, sin_s = _rope_tables(past + jnp.arange(ts))
    lg_tab = _retention_log_decay_table()
    k_pool = cache_fox_k.reshape(depth, n_pool, PAGE_SIZE * FOX_HEADS, FOX_DH)
    v_pool = cache_fox_v.reshape(depth, n_pool, PAGE_SIZE * FOX_HEADS, FOX_DH)
    lf_pages = cache_fox_logf.reshape(depth * n_pool, PAGE_SIZE * FOX_HEADS)
    lf_pool = lf_pages.reshape(depth, n_pool, 1, PAGE_SIZE * FOX_HEADS)
    sf_pool = _page_suffix(lf_pages).reshape(depth, n_pool, 1, PAGE_SIZE * FOX_HEADS)
    mem_k_s = cache_mem_k.reshape(depth, bs, n_mem, MEM_W)
    mem_v_s = cache_mem_v.reshape(depth, bs, n_mem, MEM_W)
    mem_prompt2 = mem_prompt.reshape(bp * n_mem, d)

    xp = x_prompt.reshape(bp * t, d)
    xs = jnp.zeros((rows_s, d), F32).at[:bs].set(x_sample.reshape(bs, d))
    fk_stack = jnp.zeros((depth, bp * t, FOX_HEADS, FOX_DH), F32)
    fv_stack = jnp.zeros((depth, bp * t, FOX_HEADS, FOX_DH), F32)
    outs = {k: [] for k in ("fl_p", "sg_p", "sr_p", "mk_p", "mv_p", "fk_s", "fv_s", "fl_s", "sg_s", "sr_s")}
    to_bf16_rows = lambda a: jnp.zeros((rows_s, a.size // bs), BF16).at[:bs].set(a.reshape(bs, -1).astype(BF16))

    for l in range(depth):
        lw = dict(
            w_branch=w_branch[l].astype(BF16), w_out=w_out[l].astype(BF16), g_mlp=g_mlp[l],
            w_ff1=w_ff1[l].astype(BF16), w_ff2=w_ff2[l].astype(BF16))
        w_in_p = _pack_w_in(w_in, l)
        w2p = _pack_w_gla_a2(w_gla_a2[l])

        z, zb = _matmul(xp, w_in_p, g=g_mix[l], bf16_copy=True, name="in_proj")
        z3 = z.reshape(bp, t, N_PACK)
        zb3 = zb.reshape(bp, t, N_PACK)
        ff_t = jnp.swapaxes(z3[:, :, OFF_SM + SM_FF:OFF_SM + SM_FF + FOX_HEADS], 1, 2)
        lf_t, c_t = _fox_gate(ff_t, b_fox_f[l])
        o_fox = _fox_attn(z3, zb3, c_t)
        mkv = _matmul(mem_prompt2, w_mem_kv[l].astype(BF16), name="mem_kv")
        mkv4 = mkv.reshape(1, bp, n_mem, 2 * MEM_W)
        o_mem = _mem_attn(z3, mkv4, mkv4, layer=0, kcol=0, vcol=1, out_dtype=BF16)
        o_gla, s_gla = _gla_prompt(z3, w2p, b_gla_a[l], g_gla[l])
        o_ret, s_ret = _ret_prompt(z3, cos_p, sin_p, lg_tab, g_ret[l])
        flat = lambda a: a.reshape(bp * t, -1)
        xp = _dense_tail(xp, [flat(o_fox), flat(o_gla), flat(o_ret), flat(o_mem)], z, lw)
        fk_stack, fv_stack = _heads_out(z, fk_stack, fv_stack, l)
        outs["fl_p"].append(jnp.swapaxes(lf_t, 1, 2))
        outs["sg_p"].append(s_gla)
        outs["sr_p"].append(s_ret)
        outs["mk_p"].append(mkv4[0, :, :, :MEM_W].reshape(bp, n_mem, MEM_HEADS, MEM_DH))
        outs["mv_p"].append(mkv4[0, :, :, MEM_W:].reshape(bp, n_mem, MEM_HEADS, MEM_DH))

        zs = _matmul(xs, w_in_p, g=g_mix[l], name="in_proj_s")
        zs3 = zs[:bs].reshape(bs, 1, N_PACK)
        ff_col = zs[:bs, OFF_SM + SM_FF:OFF_SM + SM_FF + FOX_HEADS].reshape(bs, FOX_HEADS, 1)
        heads = lambda off: zs[:bs, off:off + FOX_W].reshape(bs, FOX_HEADS, FOX_DH)
        o_fox_s, lf_s = _fox_decode(page_table, heads(OFF_FQ), heads(OFF_FK), heads(OFF_FV), ff_col,
                                    b_fox_f[l].reshape(FOX_HEADS, 1), k_pool, v_pool, lf_pool, sf_pool, l)
        o_mem_s = _mem_attn(zs3, mem_k_s, mem_v_s, layer=l, kcol=0, vcol=0, out_dtype=F32)
        o_gla_s, sg_s = _gla_step(zs3, w2p, b_gla_a[l], g_gla[l], state_gla, l)
        o_ret_s, sr_s = _ret_step(zs3, cos_s, sin_s, lg_tab, g_ret[l], state_ret, l)
        xs = _dense_tail(xs, [to_bf16_rows(a) for a in (o_fox_s, o_gla_s, o_ret_s, o_mem_s)], zs, lw)
        outs["fk_s"].append(zs3[:, :, OFF_FK:OFF_FK + FOX_W].reshape(bs, 1, FOX_HEADS, FOX_DH))
        outs["fv_s"].append(zs3[:, :, OFF_FV:OFF_FV + FOX_W].reshape(bs, 1, FOX_HEADS, FOX_DH))
        outs["fl_s"].append(lf_s.reshape(bs, 1, FOX_HEADS))
        outs["sg_s"].append(sg_s)
        outs["sr_s"].append(sr_s)

    y_prompt = _rmsnorm(xp, g_final).reshape(bp, t, d)
    y_sample = _rmsnorm(xs, g_final)[:bs].reshape(bs, 1, d)
    st = lambda k: jnp.stack(outs[k])
    fk_p = fk_stack.reshape(depth, bp, t, FOX_HEADS, FOX_DH)
    fv_p = fv_stack.reshape(depth, bp, t, FOX_HEADS, FOX_DH)
    return (y_prompt, y_sample, fk_p, fv_p, st("fl_p"), st("sg_p"), st("sr_p"), st("mk_p"), st("mv_p"),
            st("fk_s"), st("fv_s"), st("fl_s"), st("sg_s"), st("sr_s"))
```

```python
import functools

import numpy as np
import jax
import jax.numpy as jnp
from jax import lax
from jax.experimental import pallas as pl
from jax.experimental.pallas import tpu as pltpu

F32 = jnp.float32
BF16 = jnp.bfloat16

D_MODEL = 2048
FOX_HEADS = 8
FOX_DH = D_MODEL // 16
FOX_W = FOX_HEADS * FOX_DH
GLA_HEADS = 4
GLA_DK = D_MODEL // 16
GLA_DV = D_MODEL // 8
GLA_KW = GLA_HEADS * GLA_DK
GLA_VW = GLA_HEADS * GLA_DV
GLA_RANK = 16
GLA_TAU = 16.0
RET_HEADS = 4
RET_DH = D_MODEL // 8
RET_W = RET_HEADS * RET_DH
MEM_HEADS = 4
MEM_DH = D_MODEL // 8
MEM_W = MEM_HEADS * MEM_DH
N_BRANCH = 4
BRANCH_W = D_MODEL // 2
D_FF = 4 * D_MODEL
PAGE_SIZE = 128
ROPE_BASE = 10000.0
EPS = 1e-6
LOG2E = 1.4426950408889634

LANES = 128
SUBLANES = 8
BF16_ROWS = 16
VMEM_LIMIT_BYTES = 56 * 1024 * 1024

OFF_FQ = 0
OFF_FK = OFF_FQ + FOX_W
OFF_FV = OFF_FK + FOX_W
OFF_GQ = OFF_FV + FOX_W
OFF_GK = OFF_GQ + GLA_KW
OFF_GV = OFF_GK + GLA_KW
OFF_GR = OFF_GV + GLA_VW
OFF_RQ = OFF_GR + GLA_VW
OFF_RK = OFF_RQ + RET_W
OFF_RV = OFF_RK + RET_W
OFF_RG = OFF_RV + RET_W
OFF_MQ = OFF_RG + RET_W
OFF_GT = OFF_MQ + MEM_W
OFF_SM = OFF_GT + N_BRANCH * D_MODEL
SM_FF = 0
SM_GA = FOX_HEADS
N_PACK = OFF_SM + LANES

GLA_CHUNK = 64
GLA_SUB = 8
GLA_SUB_LOG2 = 3
GLA_CHUNKS_PER_STEP = 4
FOX_PAGES_PER_STEP = 8
FOX_DH_LOG2 = 7
assert GLA_SUB == 1 << GLA_SUB_LOG2 and FOX_DH == 1 << FOX_DH_LOG2 and LANES & (LANES - 1) == 0
RET_CHUNK = 256
REPACK_ROWS = 1024


def _cparams(sem):
    return pltpu.CompilerParams(dimension_semantics=sem, vmem_limit_bytes=VMEM_LIMIT_BYTES)


def _log_sigmoid(x):
    return jnp.minimum(x, 0.0) - jnp.log1p(jnp.exp(-jnp.abs(x)))


def _sigmoid(x):
    return 1.0 / (1.0 + jnp.exp(-x))


def _dot(a, b):
    return jnp.dot(a, b, preferred_element_type=F32)


def _dot_nt(a, b):
    return lax.dot_general(a, b, (((1,), (1,)), ((), ())), preferred_element_type=F32)


def _dot_tn(a, b):
    return lax.dot_general(a, b, (((0,), (0,)), ((), ())), preferred_element_type=F32)


def _dot01(m01, x):
    hi = x.astype(BF16)
    r1 = x - hi.astype(F32)
    mid = r1.astype(BF16)
    lo = (r1 - mid.astype(F32)).astype(BF16)
    return _dot(m01, hi) + _dot(m01, mid) + _dot(m01, lo)


def _row_to_cols(row):
    return jnp.broadcast_to(row, (LANES, LANES)).T


def _head_norm_gate(o, gain, gate):
    mu = jnp.mean(o, axis=1, keepdims=True)
    d = o - mu
    var = jnp.mean(d * d, axis=1, keepdims=True)
    y = d * lax.rsqrt(var + EPS) * gain
    return y * (gate * _sigmoid(gate))


def _mm_body(*refs, norm, act, has_res, stage, bf16_copy, w_rows):
    it = iter(refs)
    x_ref = next(it)
    g_ref = next(it) if norm else None
    w_ref = next(it)
    r_ref = next(it) if has_res else None
    o_ref = next(it)
    ob_ref = next(it) if bf16_copy else None
    h_ref = next(it) if stage else None
    if stage:
        @pl.when(pl.program_id(1) == 0)
        def _():
            x = x_ref[...].astype(F32)
            if norm:
                ms = jnp.mean(x * x, axis=-1, keepdims=True)
                x = x * lax.rsqrt(ms + EPS) * g_ref[...]
            h_ref[...] = x.astype(BF16)

        h = h_ref[...]
    else:
        h = x_ref[...]
    acc = _dot_nt(h, w_ref[...]) if w_rows else _dot(h, w_ref[...])
    if act == "relu2":
        acc = jnp.square(jnp.maximum(acc, 0.0))
    if has_res:
        acc = acc + r_ref[...]
    o_ref[...] = acc.astype(o_ref.dtype)
    if bf16_copy:
        ob_ref[...] = acc.astype(BF16)


def _pick_tile(n, candidates):
    for c in candidates:
        if n % c == 0:
            return c
    return n


def _matmul(x, w, *, layer=None, w_rows=False, g=None, res=None, act=None, out_dtype=F32, bf16_copy=False,
            name="proj"):
    m, k = x.shape
    n = w.shape[0] if w_rows else w.shape[-1]
    norm = g is not None
    stage = norm or x.dtype != BF16
    if m >= 1024:
        tm = 1024 if k <= 2048 else 512
    else:
        tm = m
    tn = _pick_tile(n, (1152, 1024, 512))
    if k > 2048:
        tn = min(tn, 512)
    assert m % tm == 0 and n % tn == 0 and tm % BF16_ROWS == 0
    in_specs = [pl.BlockSpec((tm, k), lambda i, j: (i, 0))]
    args = [x]
    if norm:
        in_specs.append(pl.BlockSpec((1, k), lambda i, j: (0, 0)))
        args.append(g.reshape(1, k).astype(F32))
    if w_rows:
        in_specs.append(pl.BlockSpec((tn, k), lambda i, j: (j, 0)))
    elif layer is None:
        in_specs.append(pl.BlockSpec((k, tn), lambda i, j: (0, j)))
    else:
        in_specs.append(pl.BlockSpec((None, k, tn), lambda i, j: (layer, 0, j)))
    args.append(w)
    if res is not None:
        in_specs.append(pl.BlockSpec((tm, tn), lambda i, j: (i, j)))
        args.append(res)
    scratch = [pltpu.VMEM((tm, k), BF16)] if stage else []
    o_spec = pl.BlockSpec((tm, tn), lambda i, j: (i, j))
    o_shape = jax.ShapeDtypeStruct((m, n), out_dtype)
    return pl.pallas_call(
        functools.partial(_mm_body, norm=norm, act=act, has_res=res is not None, stage=stage, bf16_copy=bf16_copy,
                          w_rows=w_rows),
        grid=(m // tm, n // tn),
        in_specs=in_specs,
        out_specs=[o_spec, o_spec] if bf16_copy else o_spec,
        out_shape=[o_shape, jax.ShapeDtypeStruct((m, n), BF16)] if bf16_copy else o_shape,
        scratch_shapes=scratch,
        compiler_params=_cparams(("parallel", "arbitrary")),
        name=name,
    )(*args)


def _rmsnorm_body(x_ref, g_ref, o_ref):
    x = x_ref[...]
    ms = jnp.mean(x * x, axis=-1, keepdims=True)
    o_ref[...] = x * lax.rsqrt(ms + EPS) * g_ref[...]


def _rmsnorm(x, g):
    m, k = x.shape
    tm = _pick_tile(m, (512,))
    return pl.pallas_call(
        _rmsnorm_body,
        grid=(m // tm,),
        in_specs=[pl.BlockSpec((tm, k), lambda i: (i, 0)), pl.BlockSpec((1, k), lambda i: (0, 0))],
        out_specs=pl.BlockSpec((tm, k), lambda i: (i, 0)),
        out_shape=jax.ShapeDtypeStruct((m, k), F32),
        compiler_params=_cparams(("parallel",)),
        name="final_norm",
    )(x, g.reshape(1, k))


def _fox_gate_body(x_ref, b_ref, lf_ref, c_ref):
    lf = _log_sigmoid(x_ref[...] + b_ref[...])
    lf_ref[...] = lf
    t = lf.shape[1]
    lane = lax.broadcasted_iota(jnp.int32, lf.shape, 1)
    c = lf
    shift = 1
    while shift < t:
        c = c + jnp.where(lane >= shift, pltpu.roll(c, shift, axis=1), 0.0)
        shift *= 2
    c_ref[...] = c


def _fox_gate(ff_t, bias):
    b, h, t = ff_t.shape
    spec = pl.BlockSpec((None, h, t), lambda i: (i, 0, 0))
    return pl.pallas_call(
        _fox_gate_body,
        grid=(b,),
        in_specs=[spec, pl.BlockSpec((h, 1), lambda i: (0, 0))],
        out_specs=[spec, spec],
        out_shape=[jax.ShapeDtypeStruct((b, h, t), F32)] * 2,
        compiler_params=_cparams(("parallel",)),
        name="fox_gate",
    )(ff_t, bias.reshape(h, 1))


def _fox_attn_body(q_ref, k_ref, v_ref, ck_ref, o_ref, q_sc, m_sc, l_sc, acc_sc, *, tile):
    qi = pl.program_id(1)
    ki = pl.program_id(2)
    nk = pl.num_programs(2)

    @pl.when(ki == 0)
    def _():
        q_sc[...] = (q_ref[...] * (FOX_DH ** -0.5 * LOG2E)).astype(BF16)
        m_sc[...] = jnp.full(m_sc.shape, -jnp.inf, F32)
        l_sc[...] = jnp.zeros(l_sc.shape, F32)
        acc_sc[...] = jnp.zeros(acc_sc.shape, F32)

    def step(on_diagonal):
        ck = ck_ref[...] * LOG2E
        if on_diagonal:
            row = lax.broadcasted_iota(jnp.int32, (tile, tile), 0)
            col = lax.broadcasted_iota(jnp.int32, (tile, tile), 1)
            visible = col <= row
        for h in range(FOX_HEADS):
            sl = slice(h * FOX_DH, (h + 1) * FOX_DH)
            s = _dot_nt(q_sc[:, sl], k_ref[:, sl]) - ck[h:h + 1, :]
            if on_diagonal:
                s = jnp.where(visible, s, -jnp.inf)
            m_prev = m_sc[h]
            m_new = jnp.maximum(m_prev, jnp.max(s, axis=1, keepdims=True))
            alpha = jnp.exp2(m_prev - m_new)
            p = jnp.exp2(s - m_new[:, :1])
            l_sc[h] = alpha * l_sc[h] + jnp.sum(p, axis=1, keepdims=True)
            acc_sc[:, sl] = alpha * acc_sc[:, sl] + _dot(p.astype(BF16), v_ref[:, sl])
            m_sc[h] = m_new

    pl.when(ki < qi)(functools.partial(step, False))
    pl.when(ki == qi)(functools.partial(step, True))

    @pl.when(ki == nk - 1)
    def _():
        for h in range(FOX_HEADS):
            sl = slice(h * FOX_DH, (h + 1) * FOX_DH)
            o_ref[:, sl] = (acc_sc[:, sl] / l_sc[h]).astype(o_ref.dtype)


def _fox_attn(z3, zb3, c_row):
    b, t, _ = z3.shape
    tile = _pick_tile(t, (512, 256, 128))
    n = t // tile
    qcol, kcol, vcol = OFF_FQ // FOX_W, OFF_FK // FOX_W, OFF_FV // FOX_W
    return pl.pallas_call(
        functools.partial(_fox_attn_body, tile=tile),
        grid=(b, n, n),
        in_specs=[
            pl.BlockSpec((None, tile, FOX_W), lambda bi, qi, ki: (bi, qi, qcol)),
            pl.BlockSpec((None, tile, FOX_W), lambda bi, qi, ki: (bi, jnp.minimum(ki, qi), kcol)),
            pl.BlockSpec((None, tile, FOX_W), lambda bi, qi, ki: (bi, jnp.minimum(ki, qi), vcol)),
            pl.BlockSpec((None, FOX_HEADS, tile), lambda bi, qi, ki: (bi, 0, jnp.minimum(ki, qi))),
        ],
        out_specs=pl.BlockSpec((None, tile, FOX_W), lambda bi, qi, ki: (bi, qi, 0)),
        out_shape=jax.ShapeDtypeStruct((b, t, FOX_W), BF16),
        scratch_shapes=[
            pltpu.VMEM((tile, FOX_W), BF16),
            pltpu.VMEM((FOX_HEADS, tile, FOX_DH), F32),
            pltpu.VMEM((FOX_HEADS, tile, FOX_DH), F32),
            pltpu.VMEM((tile, FOX_W), F32),
        ],
        compiler_params=_cparams(("parallel", "parallel", "arbitrary")),
        name="fox_attn",
    )(z3, zb3, zb3, c_row)


def _mem_attn_body(q_ref, k_ref, v_ref, o_ref, *, rows):
    scale = MEM_DH ** -0.5
    for h in range(MEM_HEADS):
        sl = slice(h * MEM_DH, (h + 1) * MEM_DH)
        q = q_ref[:, sl]
        if rows < SUBLANES:
            q = jnp.broadcast_to(q, (SUBLANES, MEM_DH))
        s = _dot_nt(q.astype(BF16), k_ref[:, sl].astype(BF16)) * scale
        m = jnp.max(s, axis=1, keepdims=True)
        p = jnp.exp(s - m)
        p = p / jnp.sum(p, axis=1, keepdims=True)
        o = _dot(p.astype(BF16), v_ref[:, sl].astype(BF16))
        o_ref[:, sl] = o[:rows].astype(o_ref.dtype)


def _mem_attn(z3, k4, v4, *, layer, kcol, vcol, out_dtype):
    b, t, _ = z3.shape
    n_mem = k4.shape[2]
    tq = _pick_tile(t, (512, 256, 128))
    qcol = OFF_MQ // MEM_W
    return pl.pallas_call(
        functools.partial(_mem_attn_body, rows=tq),
        grid=(b, t // tq),
        in_specs=[
            pl.BlockSpec((None, tq, MEM_W), lambda bi, qi: (bi, qi, qcol)),
            pl.BlockSpec((None, None, n_mem, MEM_W), lambda bi, qi: (layer, bi, 0, kcol)),
            pl.BlockSpec((None, None, n_mem, MEM_W), lambda bi, qi: (layer, bi, 0, vcol)),
        ],
        out_specs=pl.BlockSpec((None, tq, MEM_W), lambda bi, qi: (bi, qi, 0)),
        out_shape=jax.ShapeDtypeStruct((b, t, MEM_W), out_dtype),
        compiler_params=_cparams(("parallel", "parallel")),
        name="mem_attn",
    )(z3, k4, v4)


def _rotary(x, cos, sin):
    half = x.shape[1] // 2
    x1, x2 = x[:, :half], x[:, half:]
    return jnp.concatenate([x1 * cos - x2 * sin, x1 * sin + x2 * cos], axis=1)


def _ret_body(q_ref, k_ref, v_ref, g_ref, cos_ref, sin_ref, lg_ref, gn_ref, o_ref, s_ref, st_sc, *, chunk):
    c = pl.program_id(1)
    nc = pl.num_programs(1)

    @pl.when(c == 0)
    def _():
        st_sc[...] = jnp.zeros(st_sc.shape, F32)

    cos, sin = cos_ref[...], sin_ref[...]
    row = lax.broadcasted_iota(jnp.int32, (chunk, chunk), 0)
    col = lax.broadcasted_iota(jnp.int32, (chunk, chunk), 1)
    lag = (row - col).astype(F32)
    t = lax.broadcasted_iota(jnp.int32, (chunk, 1), 0).astype(F32)
    for h in range(RET_HEADS):
        sl = slice(h * RET_DH, (h + 1) * RET_DH)
        lg = lg_ref[h][:, :1]
        q = _rotary(q_ref[:, sl], cos, sin)
        k = _rotary(k_ref[:, sl], cos, sin) * (RET_DH ** -0.5)
        v = v_ref[:, sl].astype(BF16)
        decay = jnp.where(col <= row, jnp.exp(lag * lg), 0.0)
        a = _dot_nt(q.astype(BF16), k.astype(BF16)) * decay
        s0 = st_sc[h]
        o = _dot(a.astype(BF16), v) + jnp.exp((t + 1.0) * lg) * _dot(q.astype(BF16), s0.astype(BF16))
        k_end = k * jnp.exp((chunk - 1.0 - t) * lg)
        s_new = jnp.exp(chunk * lg) * s0 + _dot_tn(k_end.astype(BF16), v)
        st_sc[h] = s_new
        o_ref[:, sl] = _head_norm_gate(o, gn_ref[:, sl], g_ref[:, sl]).astype(o_ref.dtype)

        @pl.when(c == nc - 1)
        def _():
            s_ref[h] = s_new


def _ret_prompt(z3, cos, sin, lg_tab, g_ret):
    b, t, _ = z3.shape
    chunk = _pick_tile(t, (RET_CHUNK, 128, 64))
    zspec = lambda off: pl.BlockSpec((None, chunk, RET_W), lambda bi, c: (bi, c, off // RET_W))
    return pl.pallas_call(
        functools.partial(_ret_body, chunk=chunk),
        grid=(b, t // chunk),
        in_specs=[
            zspec(OFF_RQ), zspec(OFF_RK), zspec(OFF_RV), zspec(OFF_RG),
            pl.BlockSpec((chunk, RET_DH // 2), lambda bi, c: (c, 0)),
            pl.BlockSpec((chunk, RET_DH // 2), lambda bi, c: (c, 0)),
            pl.BlockSpec((RET_HEADS, 1, LANES), lambda bi, c: (0, 0, 0)),
            pl.BlockSpec((1, RET_W), lambda bi, c: (0, 0)),
        ],
        out_specs=[
            pl.BlockSpec((None, chunk, RET_W), lambda bi, c: (bi, c, 0)),
            pl.BlockSpec((None, RET_HEADS, RET_DH, RET_DH), lambda bi, c: (bi, 0, 0, 0)),
        ],
        out_shape=[
            jax.ShapeDtypeStruct((b, t, RET_W), BF16),
            jax.ShapeDtypeStruct((b, RET_HEADS, RET_DH, RET_DH), F32),
        ],
        scratch_shapes=[pltpu.VMEM((RET_HEADS, RET_DH, RET_DH), F32)],
        compiler_params=_cparams(("parallel", "arbitrary")),
        name="ret_prompt",
    )(z3, z3, z3, z3, cos, sin, lg_tab, g_ret.reshape(1, RET_W))


def _gla_loga(zs, w2_ref, ba_ref):
    pre = _dot(zs.astype(BF16), w2_ref[...]) + ba_ref[...]
    return _log_sigmoid(pre) * (1.0 / GLA_TAU)


def _gla_chunk(q, k, v_ref, gr_ref, gn_ref, o_ref, loga, st, tri, hsum, rows):
    chunk = q.shape[0]
    n_sub = chunk // GLA_SUB
    b = _dot01(tri, loga)
    row = lax.broadcasted_iota(jnp.int32, (chunk, GLA_KW), 0)
    b_last = b[chunk - 1:chunk, :]
    beta = jnp.zeros_like(b)
    for i in range(1, n_sub):
        beta = jnp.where(row >= i * GLA_SUB, b[i * GLA_SUB - 1:i * GLA_SUB, :], beta)
    q_rel = q * jnp.exp(b - beta)
    q_abs = q * jnp.exp(b)
    k_end = k * jnp.exp(b_last - b)
    k_rel = []
    for i in range(1, n_sub):
        bi = b[i * GLA_SUB - 1:i * GLA_SUB, :]
        k_rel.append(jnp.where(row < i * GLA_SUB, k * jnp.exp(jnp.minimum(bi - b, 0.0)), 0.0).astype(BF16))

    prods = []
    for delta in range(GLA_SUB):
        kd = k if delta == 0 else pltpu.roll(k, delta, axis=0)
        bd = b if delta == 0 else pltpu.roll(b, delta, axis=0)
        prods.append((q * kd * jnp.exp(jnp.minimum(b - bd, 0.0))).astype(BF16))
    sums = _dot(jnp.concatenate(prods, axis=0), hsum)
    rowb = lax.broadcasted_iota(jnp.int32, (chunk, GLA_HEADS * LANES), 0)
    colb = lax.broadcasted_iota(jnp.int32, (chunk, GLA_HEADS * LANES), 1) & (LANES - 1)
    rsub = rowb & (GLA_SUB - 1)
    band = jnp.zeros((chunk, GLA_HEADS * LANES), F32)
    for delta in range(GLA_SUB):
        keep = jnp.logical_and(colb == rowb - delta, rsub >= delta)
        band = band + jnp.where(keep, sums[delta * chunk:(delta + 1) * chunk], 0.0)

    sub_of_row = row[:, :GLA_DK] >> GLA_SUB_LOG2
    new_st = []
    for h in range(GLA_HEADS):
        ks = slice(h * GLA_DK, (h + 1) * GLA_DK)
        vs = slice(h * GLA_DV, (h + 1) * GLA_DV)
        v = v_ref[rows, vs].astype(BF16)
        lhs = jnp.concatenate(
            [jnp.where(sub_of_row == i, q_rel[:, ks], 0.0) for i in range(1, n_sub)], axis=1).astype(BF16)
        rhs = jnp.concatenate([kr[:, ks] for kr in k_rel], axis=1)
        a = _dot_nt(lhs, rhs) + band[:, h * LANES:h * LANES + chunk]
        o = _dot(a.astype(BF16), v) + _dot_nt(q_abs[:, ks].astype(BF16), st[h].astype(BF16))
        new_st.append(st[h] * jnp.exp(b_last[:, ks]) + _dot_tn(v, k_end[:, ks].astype(BF16)))
        o_ref[rows, vs] = _head_norm_gate(o, gn_ref[:, vs], gr_ref[rows, vs]).astype(o_ref.dtype)
    return new_st


def _gla_body(q_ref, k_ref, v_ref, gr_ref, zs_ref, w2_ref, ba_ref, gn_ref, tri_ref, hsum_ref,
              o_ref, s_ref, st_sc, *, chunk, n_chunks):
    c = pl.program_id(1)
    nc = pl.num_programs(1)

    @pl.when(c == 0)
    def _():
        st_sc[...] = jnp.zeros(st_sc.shape, F32)

    loga = _gla_loga(zs_ref[...], w2_ref, ba_ref)
    tri, hsum = tri_ref[...], hsum_ref[...]
    st = [st_sc[h] for h in range(GLA_HEADS)]
    for ci in range(n_chunks):
        rows = slice(ci * chunk, (ci + 1) * chunk)
        q = q_ref[rows, :] * (GLA_DK ** -0.5)
        st = _gla_chunk(q, k_ref[rows, :], v_ref, gr_ref, gn_ref, o_ref, loga[rows], st, tri, hsum, rows)
    for h in range(GLA_HEADS):
        st_sc[h] = st[h]

    @pl.when(c == nc - 1)
    def _():
        for h in range(GLA_HEADS):
            s_ref[h] = st[h].T


def _gla_consts(chunk):
    r = np.arange(chunk)
    tri = (r[None, :] <= r[:, None]).astype(np.float32)
    lane = np.arange(GLA_KW)
    out = np.arange(GLA_HEADS * LANES)
    hsum = (lane[:, None] // GLA_DK == out[None, :] // LANES).astype(np.float32)
    return jnp.asarray(tri, BF16), jnp.asarray(hsum, BF16)


def _gla_prompt(z3, w2p, b_a, g_gla):
    b, t, _ = z3.shape
    chunk = GLA_CHUNK
    n_chunks = _pick_tile(t // chunk, (GLA_CHUNKS_PER_STEP, 2, 1))
    rows = chunk * n_chunks
    assert t % rows == 0
    tri, hsum = _gla_consts(chunk)
    full = lambda shape: pl.BlockSpec(shape, lambda bi, c: (0,) * len(shape))
    return pl.pallas_call(
        functools.partial(_gla_body, chunk=chunk, n_chunks=n_chunks),
        grid=(b, t // rows),
        in_specs=[
            pl.BlockSpec((None, rows, GLA_KW), lambda bi, c: (bi, c, OFF_GQ // GLA_KW)),
            pl.BlockSpec((None, rows, GLA_KW), lambda bi, c: (bi, c, OFF_GK // GLA_KW)),
            pl.BlockSpec((None, rows, GLA_VW), lambda bi, c: (bi, c, OFF_GV // GLA_VW)),
            pl.BlockSpec((None, rows, GLA_VW), lambda bi, c: (bi, c, OFF_GR // GLA_VW)),
            pl.BlockSpec((None, rows, LANES), lambda bi, c: (bi, c, OFF_SM // LANES)),
            full((LANES, GLA_KW)), full((1, GLA_KW)), full((1, GLA_VW)),
            full((chunk, chunk)), full((GLA_KW, GLA_HEADS * LANES)),
        ],
        out_specs=[
            pl.BlockSpec((None, rows, GLA_VW), lambda bi, c: (bi, c, 0)),
            pl.BlockSpec((None, GLA_HEADS, GLA_DK, GLA_DV), lambda bi, c: (bi, 0, 0, 0)),
        ],
        out_shape=[
            jax.ShapeDtypeStruct((b, t, GLA_VW), BF16),
            jax.ShapeDtypeStruct((b, GLA_HEADS, GLA_DK, GLA_DV), F32),
        ],
        scratch_shapes=[pltpu.VMEM((GLA_HEADS, GLA_DV, GLA_DK), F32)],
        compiler_params=_cparams(("parallel", "arbitrary")),
        name="gla_prompt",
    )(z3, z3, z3, z3, z3, w2p, b_a.reshape(1, GLA_KW), g_gla.reshape(1, GLA_VW), tri, hsum)


def _gla_step_body(q_ref, k_ref, v_ref, gr_ref, zs_ref, w2_ref, ba_ref, gn_ref, s0_ref, o_ref, s_ref):
    zs = jnp.broadcast_to(zs_ref[...], (BF16_ROWS, LANES))
    loga = _gla_loga(zs, w2_ref, ba_ref)[:1]
    q = q_ref[...] * (GLA_DK ** -0.5)
    k = k_ref[...]
    for h in range(GLA_HEADS):
        ks = slice(h * GLA_DK, (h + 1) * GLA_DK)
        vs = slice(h * GLA_DV, (h + 1) * GLA_DV)
        widen = lambda m: jnp.concatenate([m] * (GLA_DV // LANES), axis=1)
        a_c = widen(_row_to_cols(jnp.exp(loga[:, ks])))
        k_c = widen(_row_to_cols(k[:, ks]))
        q_c = widen(_row_to_cols(q[:, ks]))
        s_new = a_c * s0_ref[h] + k_c * v_ref[:, vs]
        s_ref[h] = s_new
        o = jnp.sum(q_c * s_new, axis=0, keepdims=True)
        o_ref[:, vs] = _head_norm_gate(o, gn_ref[:, vs], gr_ref[:, vs])


def _gla_step(zs3, w2p, b_a, g_gla, state, layer):
    b = zs3.shape[0]
    full = lambda shape: pl.BlockSpec(shape, lambda bi: (0,) * len(shape))
    return pl.pallas_call(
        _gla_step_body,
        grid=(b,),
        in_specs=[
            pl.BlockSpec((None, 1, GLA_KW), lambda bi: (bi, 0, OFF_GQ // GLA_KW)),
            pl.BlockSpec((None, 1, GLA_KW), lambda bi: (bi, 0, OFF_GK // GLA_KW)),
            pl.BlockSpec((None, 1, GLA_VW), lambda bi: (bi, 0, OFF_GV // GLA_VW)),
            pl.BlockSpec((None, 1, GLA_VW), lambda bi: (bi, 0, OFF_GR // GLA_VW)),
            pl.BlockSpec((None, 1, LANES), lambda bi: (bi, 0, OFF_SM // LANES)),
            full((LANES, GLA_KW)), full((1, GLA_KW)), full((1, GLA_VW)),
            pl.BlockSpec((None, None, GLA_HEADS, GLA_DK, GLA_DV), lambda bi: (layer, bi, 0, 0, 0)),
        ],
        out_specs=[
            pl.BlockSpec((None, 1, GLA_VW), lambda bi: (bi, 0, 0)),
            pl.BlockSpec((None, GLA_HEADS, GLA_DK, GLA_DV), lambda bi: (bi, 0, 0, 0)),
        ],
        out_shape=[
            jax.ShapeDtypeStruct((b, 1, GLA_VW), F32),
            jax.ShapeDtypeStruct((b, GLA_HEADS, GLA_DK, GLA_DV), F32),
        ],
        compiler_params=_cparams(("parallel",)),
        name="gla_step",
    )(zs3, zs3, zs3, zs3, zs3, w2p, b_a.reshape(1, GLA_KW), g_gla.reshape(1, GLA_VW), state)


def _ret_step_body(q_ref, k_ref, v_ref, g_ref, cos_ref, sin_ref, lg_ref, gn_ref, s0_ref, o_ref, s_ref):
    cos, sin = cos_ref[...], sin_ref[...]
    half = RET_DH // 2
    for h in range(RET_HEADS):
        sl = slice(h * RET_DH, (h + 1) * RET_DH)
        q = _rotary(q_ref[:, sl], cos, sin)
        k = _rotary(k_ref[:, sl], cos, sin) * (RET_DH ** -0.5)
        cols = lambda r: jnp.concatenate(
            [jnp.concatenate([_row_to_cols(r[:, i * half:(i + 1) * half])] * 2, axis=1) for i in range(2)], axis=0)
        gamma = jnp.exp(lg_ref[h][:, :1])
        s_new = gamma * s0_ref[h] + cols(k) * v_ref[:, sl]
        s_ref[h] = s_new
        o = jnp.sum(cols(q) * s_new, axis=0, keepdims=True)
        o_ref[:, sl] = _head_norm_gate(o, gn_ref[:, sl], g_ref[:, sl])


def _ret_step(zs3, cos, sin, lg_tab, g_ret, state, layer):
    b = zs3.shape[0]
    full = lambda shape: pl.BlockSpec(shape, lambda bi: (0,) * len(shape))
    zspec = lambda off: pl.BlockSpec((None, 1, RET_W), lambda bi: (bi, 0, off // RET_W))
    return pl.pallas_call(
        _ret_step_body,
        grid=(b,),
        in_specs=[
            zspec(OFF_RQ), zspec(OFF_RK), zspec(OFF_RV), zspec(OFF_RG),
            full((1, RET_DH // 2)), full((1, RET_DH // 2)), full((RET_HEADS, 1, LANES)), full((1, RET_W)),
            pl.BlockSpec((None, None, RET_HEADS, RET_DH, RET_DH), lambda bi: (layer, bi, 0, 0, 0)),
        ],
        out_specs=[
            pl.BlockSpec((None, 1, RET_W), lambda bi: (bi, 0, 0)),
            pl.BlockSpec((None, RET_HEADS, RET_DH, RET_DH), lambda bi: (bi, 0, 0, 0)),
        ],
        out_shape=[
            jax.ShapeDtypeStruct((b, 1, RET_W), F32),
            jax.ShapeDtypeStruct((b, RET_HEADS, RET_DH, RET_DH), F32),
        ],
        compiler_params=_cparams(("parallel",)),
        name="ret_step",
    )(zs3, zs3, zs3, zs3, cos, sin, lg_tab, g_ret.reshape(1, RET_W), state)


def _page_suffix_body(x_ref, o_ref):
    x = x_ref[...]
    width = x.shape[1]
    lane = lax.broadcasted_iota(jnp.int32, x.shape, 1)
    incl = x
    shift = FOX_HEADS
    while shift < width:
        moved = pltpu.roll(incl, width - shift, axis=1)
        incl = incl + jnp.where(lane < width - shift, moved, 0.0)
        shift *= 2
    o_ref[...] = incl


def _page_suffix(lf_pages):
    n, width = lf_pages.shape
    rows = _pick_tile(n, (64, 32, 16, 8))
    spec = pl.BlockSpec((rows, width), lambda i: (i, 0))
    return pl.pallas_call(
        _page_suffix_body,
        grid=(n // rows,),
        in_specs=[spec],
        out_specs=spec,
        out_shape=jax.ShapeDtypeStruct((n, width), F32),
        compiler_params=_cparams(("parallel",)),
        name="page_suffix",
    )(lf_pages)


def _fox_decode_body(*refs, pages):
    q_ref, kn_ref, vn_ref, ff_ref, bf_ref = refs[1:6]
    kp_refs = refs[6:6 + pages]
    vp_refs = refs[6 + pages:6 + 2 * pages]
    lp_refs = refs[6 + 2 * pages:6 + 3 * pages]
    sf_refs = refs[6 + 3 * pages:6 + 4 * pages]
    o_ref, lf_ref, m_sc, l_sc, tail_sc, acc_sc = refs[6 + 4 * pages:]
    g = pl.program_id(1)
    n_groups = pl.num_programs(1)
    scale = FOX_DH ** -0.5
    width = PAGE_SIZE * FOX_HEADS
    sub = lax.broadcasted_iota(jnp.int32, (FOX_HEADS, width), 0)
    lane = lax.broadcasted_iota(jnp.int32, (FOX_HEADS, width), 1)
    own = (lane & (FOX_HEADS - 1)) == sub
    q = q_ref[...]
    qb = q.astype(BF16)

    @pl.when(g == 0)
    def _():
        lf_new = _log_sigmoid(ff_ref[...] + bf_ref[...])
        lf_ref[...] = lf_new
        tail_sc[...] = jnp.broadcast_to(lf_new, tail_sc.shape)
        s_new = jnp.sum(q * kn_ref[...], axis=1, keepdims=True) * scale
        m_sc[...] = jnp.broadcast_to(s_new, m_sc.shape)
        l_sc[...] = jnp.ones(l_sc.shape, F32)
        acc_sc[...] = vn_ref[...]

    tail = tail_sc[:, :1]
    scores = []
    for i in range(pages):
        incl = jnp.broadcast_to(sf_refs[i][...], (FOX_HEADS, width))
        bias = tail + (incl - lp_refs[i][...])
        s = _dot_nt(qb, kp_refs[i][...].astype(BF16)) * scale + bias
        scores.append(jnp.where(own, s, -jnp.inf))
        tail = tail + jnp.sum(jnp.where(lane == sub, incl, 0.0), axis=1, keepdims=True)
    m_prev = m_sc[:, :1]
    m_new = m_prev
    for s in scores:
        m_new = jnp.maximum(m_new, jnp.max(s, axis=1, keepdims=True))
    alpha = jnp.exp(m_prev - m_new)
    l_new = alpha * l_sc[:, :1]
    acc = alpha * acc_sc[...]
    for i in range(pages):
        pr = jnp.exp(scores[i] - m_new)
        l_new = l_new + jnp.sum(pr, axis=1, keepdims=True)
        acc = acc + _dot(pr.astype(BF16), vp_refs[i][...].astype(BF16))
    acc_sc[...] = acc
    l_sc[...] = jnp.broadcast_to(l_new, l_sc.shape)
    m_sc[...] = jnp.broadcast_to(m_new, m_sc.shape)
    tail_sc[...] = jnp.broadcast_to(tail, tail_sc.shape)

    @pl.when(g == n_groups - 1)
    def _():
        o_ref[...] = acc / l_new


def _fox_decode(page_table, q8, kn8, vn8, ff_col, bf_col, k_pool, v_pool, lf_pool, sf_pool, layer):
    b, n_pages = page_table.shape
    width = PAGE_SIZE * FOX_HEADS
    pages = _pick_tile(n_pages, (FOX_PAGES_PER_STEP, 2, 1))

    def page(i):
        return lambda bi, g, pt: (layer, pt[bi, n_pages - 1 - (g * pages + i)], 0, 0)

    tok = pl.BlockSpec((None, FOX_HEADS, FOX_DH), lambda bi, g, pt: (bi, 0, 0))
    col = pl.BlockSpec((None, FOX_HEADS, 1), lambda bi, g, pt: (bi, 0, 0))
    kv_specs = [pl.BlockSpec((None, None, width, FOX_DH), page(i)) for i in range(pages)]
    row_specs = [pl.BlockSpec((None, None, 1, width), page(i)) for i in range(pages)]
    grid_spec = pltpu.PrefetchScalarGridSpec(
        num_scalar_prefetch=1,
        grid=(b, n_pages // pages),
        in_specs=[tok, tok, tok, col, pl.BlockSpec((FOX_HEADS, 1), lambda bi, g, pt: (0, 0))]
        + kv_specs + kv_specs + row_specs + row_specs,
        out_specs=[tok, col],
        scratch_shapes=[
            pltpu.VMEM((FOX_HEADS, LANES), F32),
            pltpu.VMEM((FOX_HEADS, LANES), F32),
            pltpu.VMEM((FOX_HEADS, LANES), F32),
            pltpu.VMEM((FOX_HEADS, FOX_DH), F32),
        ],
    )
    return pl.pallas_call(
        functools.partial(_fox_decode_body, pages=pages),
        grid_spec=grid_spec,
        out_shape=[
            jax.ShapeDtypeStruct((b, FOX_HEADS, FOX_DH), F32),
            jax.ShapeDtypeStruct((b, FOX_HEADS, 1), F32),
        ],
        compiler_params=_cparams(("parallel", "arbitrary")),
        name="fox_decode",
    )(page_table, q8, kn8, vn8, ff_col, bf_col, *([k_pool] * pages), *([v_pool] * pages),
      *([lf_pool] * pages), *([sf_pool] * pages))


def _merge_body(o0, o1, o2, o3, g0, g1, g2, g3, w_ref, out_ref):
    acc = None
    for i, (o_ref, g_ref) in enumerate(((o0, g0), (o1, g1), (o2, g2), (o3, g3))):
        y = (0.5 * jnp.tanh(0.5 * g_ref[...]) + 0.5) * _dot(o_ref[...], w_ref[i])
        acc = y if acc is None else acc + y
    out_ref[...] = acc.astype(out_ref.dtype)


def _merge(branches, z, w_branch, layer):
    m = z.shape[0]
    tm = _pick_tile(m, (512,))
    tn = 512
    assert D_MODEL % tn == 0 and OFF_GT % tn == 0
    o_spec = pl.BlockSpec((tm, BRANCH_W), lambda i, j: (i, 0))
    g_spec = lambda br: pl.BlockSpec((tm, tn), lambda i, j: (i, (OFF_GT + br * D_MODEL) // tn + j))
    return pl.pallas_call(
        _merge_body,
        grid=(m // tm, D_MODEL // tn),
        in_specs=[o_spec] * N_BRANCH + [g_spec(br) for br in range(N_BRANCH)]
        + [pl.BlockSpec((None, N_BRANCH, BRANCH_W, tn), lambda i, j: (layer, 0, 0, j))],
        out_specs=pl.BlockSpec((tm, tn), lambda i, j: (i, j)),
        out_shape=jax.ShapeDtypeStruct((m, D_MODEL), BF16),
        compiler_params=_cparams(("parallel", "arbitrary")),
        name="merge",
    )(*branches, z, z, z, z, w_branch)


def _heads_out_body(k_ref, v_ref, k_any, v_any, ko_ref, vo_ref):
    del k_any, v_any
    for h in range(FOX_HEADS):
        sl = slice(h * FOX_DH, (h + 1) * FOX_DH)
        ko_ref[:, h, :] = k_ref[:, sl]
        vo_ref[:, h, :] = v_ref[:, sl]


def _heads_out(z, k_stack, v_stack, layer):
    m = z.shape[0]
    tm = _pick_tile(m, (512,))
    o_spec = pl.BlockSpec((None, tm, FOX_HEADS, FOX_DH), lambda i: (layer, i, 0, 0))
    return pl.pallas_call(
        _heads_out_body,
        grid=(m // tm,),
        in_specs=[
            pl.BlockSpec((tm, FOX_W), lambda i: (i, OFF_FK // FOX_W)),
            pl.BlockSpec((tm, FOX_W), lambda i: (i, OFF_FV // FOX_W)),
            pl.BlockSpec(memory_space=pl.ANY),
            pl.BlockSpec(memory_space=pl.ANY),
        ],
        out_specs=[o_spec, o_spec],
        out_shape=[jax.ShapeDtypeStruct(k_stack.shape, F32), jax.ShapeDtypeStruct(v_stack.shape, F32)],
        input_output_aliases={2: 0, 3: 1},
        compiler_params=_cparams(("parallel",)),
        name="heads_out",
    )(z, z, k_stack, v_stack)


def _repack_body(src_ref, ff_ref, ga_ref, o_ref, *, n_big):
    j = pl.program_id(0)

    @pl.when(j < n_big)
    def _():
        o_ref[...] = src_ref[0].astype(BF16)

    @pl.when(j == n_big)
    def _():
        k = o_ref.shape[1]
        pad = jnp.zeros((o_ref.shape[0] - FOX_HEADS - GLA_RANK, k), F32)
        o_ref[...] = jnp.concatenate([ff_ref[0], ga_ref[0], pad], axis=0).astype(BF16)


def _pack_w_in(w_in_t, layer):
    _, d_in, k = w_in_t.shape
    c_ff = 3 * FOX_W
    c_ga = c_ff + FOX_HEADS + 2 * GLA_KW + 2 * GLA_VW
    n_big = OFF_SM // REPACK_ROWS
    assert OFF_GQ % REPACK_ROWS == 0 and OFF_RQ % REPACK_ROWS == 0 and OFF_SM % REPACK_ROWS == 0
    assert FOX_HEADS % SUBLANES == 0 and GLA_RANK % SUBLANES == 0 and SM_GA == SM_FF + FOX_HEADS

    def src_index(j):
        start = jnp.minimum(j, n_big - 1) * REPACK_ROWS
        shift = jnp.where(start < OFF_GQ, 0, jnp.where(start < OFF_RQ, FOX_HEADS, FOX_HEADS + GLA_RANK))
        return (layer, pl.multiple_of(start + shift, SUBLANES), 0)

    rows = lambda n, start: pl.BlockSpec((pl.Element(1), pl.Element(n), pl.Element(k)), lambda j: (layer, start, 0))
    return pl.pallas_call(
        functools.partial(_repack_body, n_big=n_big),
        grid=(n_big + 1,),
        in_specs=[
            pl.BlockSpec((pl.Element(1), pl.Element(REPACK_ROWS), pl.Element(k)), src_index),
            rows(FOX_HEADS, c_ff), rows(GLA_RANK, c_ga),
        ],
        out_specs=pl.BlockSpec((REPACK_ROWS, k), lambda j: (j, 0)),
        out_shape=jax.ShapeDtypeStruct((N_PACK, k), BF16),
        compiler_params=_cparams(("arbitrary",)),
        name="repack_w_in",
    )(w_in_t, w_in_t, w_in_t)


def _pack_w_gla_a2(w2):
    out = jnp.zeros((LANES, GLA_KW), F32).at[SM_GA:SM_GA + GLA_RANK].set(w2)
    return out.astype(BF16)


def _rope_tables(pos):
    half = RET_DH // 2
    inv = ROPE_BASE ** (-jnp.arange(half, dtype=F32) / half)
    ang = pos.astype(F32)[:, None] * inv[None, :]
    return jnp.cos(ang), jnp.sin(ang)


def _retention_log_decay_table():
    lg = jnp.log1p(-jnp.exp2(-5.0 - jnp.arange(RET_HEADS, dtype=F32)))
    return jnp.broadcast_to(lg[:, None, None], (RET_HEADS, 1, LANES))


def _dense_tail(x, branches, z, wb, layer):
    merged = _merge(branches, z, wb["w_branch"], layer)
    x = _matmul(merged, wb["w_out"], layer=layer, res=x, name="out_proj")
    u = _matmul(x, wb["w_ff1"], layer=layer, g=wb["g_mlp"][layer], act="relu2", out_dtype=BF16, name="ff1")
    return _matmul(u, wb["w_ff2"], layer=layer, res=x, name="ff2")


def kernel(x_prompt, x_sample, cache_fox_k, cache_fox_v, cache_fox_logf, state_gla, state_ret, cache_mem_k, cache_mem_v, page_table, mem_prompt, g_mix, w_in, b_fox_f, w_gla_a2, b_gla_a, g_gla, g_ret, w_mem_kv, w_branch, w_out, g_mlp, w_ff1, w_ff2, g_final):
    depth = w_in.shape[0]
    bp, t, d = x_prompt.shape
    bs, ts, _ = x_sample.shape
    assert ts == 1 and d == D_MODEL
    n_pool = cache_fox_k.shape[1]
    n_mem = mem_prompt.shape[1]
    past = page_table.shape[1] * PAGE_SIZE
    rows_s = -(-bs // BF16_ROWS) * BF16_ROWS

    cos_p, sin_p = _rope_tables(jnp.arange(t))
    cos_s, sin_s = _rope_tables(past + jnp.arange(ts))
    lg_tab = _retention_log_decay_table()
    k_pool = cache_fox_k.reshape(depth, n_pool, PAGE_SIZE * FOX_HEADS, FOX_DH)
    v_pool = cache_fox_v.reshape(depth, n_pool, PAGE_SIZE * FOX_HEADS, FOX_DH)
    lf_pages = cache_fox_logf.reshape(depth * n_pool, PAGE_SIZE * FOX_HEADS)
    lf_pool = lf_pages.reshape(depth, n_pool, 1, PAGE_SIZE * FOX_HEADS)
    sf_pool = _page_suffix(lf_pages).reshape(depth, n_pool, 1, PAGE_SIZE * FOX_HEADS)
    mem_k_s = cache_mem_k.reshape(depth, bs, n_mem, MEM_W)
    mem_v_s = cache_mem_v.reshape(depth, bs, n_mem, MEM_W)
    mem_prompt2 = mem_prompt.reshape(bp * n_mem, d)

    xp = x_prompt.reshape(bp * t, d)
    xs = jnp.zeros((rows_s, d), F32).at[:bs].set(x_sample.reshape(bs, d))
    fk_stack = jnp.zeros((depth, bp * t, FOX_HEADS, FOX_DH), F32)
    fv_stack = jnp.zeros((depth, bp * t, FOX_HEADS, FOX_DH), F32)
    outs = {k: [] for k in ("fl_p", "sg_p", "sr_p", "mk_p", "mv_p", "fk_s", "fv_s", "fl_s", "sg_s", "sr_s")}
    to_bf16_rows = lambda a: jnp.zeros((rows_s, a.size // bs), BF16).at[:bs].set(a.reshape(bs, -1).astype(BF16))

    wb = dict(w_branch=w_branch.astype(BF16), w_out=w_out.astype(BF16), g_mlp=g_mlp,
              w_ff1=w_ff1.astype(BF16), w_ff2=w_ff2.astype(BF16))
    w_mem_kv_b = w_mem_kv.astype(BF16)
    w_in_t = jnp.swapaxes(w_in, 1, 2)
    for l in range(depth):
        w_in_p = _pack_w_in(w_in_t, l)
        w2p = _pack_w_gla_a2(w_gla_a2[l])

        z, zb = _matmul(xp, w_in_p, w_rows=True, g=g_mix[l], bf16_copy=True, name="in_proj")
        z3 = z.reshape(bp, t, N_PACK)
        zb3 = zb.reshape(bp, t, N_PACK)
        ff_t = jnp.swapaxes(z3[:, :, OFF_SM + SM_FF:OFF_SM + SM_FF + FOX_HEADS], 1, 2)
        lf_t, c_t = _fox_gate(ff_t, b_fox_f[l])
        o_fox = _fox_attn(z3, zb3, c_t)
        mkv = _matmul(mem_prompt2, w_mem_kv_b, layer=l, name="mem_kv")
        mkv4 = mkv.reshape(1, bp, n_mem, 2 * MEM_W)
        o_mem = _mem_attn(z3, mkv4, mkv4, layer=0, kcol=0, vcol=1, out_dtype=BF16)
        o_gla, s_gla = _gla_prompt(z3, w2p, b_gla_a[l], g_gla[l])
        o_ret, s_ret = _ret_prompt(z3, cos_p, sin_p, lg_tab, g_ret[l])
        flat = lambda a: a.reshape(bp * t, -1)
        xp = _dense_tail(xp, [flat(o_fox), flat(o_gla), flat(o_ret), flat(o_mem)], z, wb, l)
        fk_stack, fv_stack = _heads_out(z, fk_stack, fv_stack, l)
        outs["fl_p"].append(jnp.swapaxes(lf_t, 1, 2))
        outs["sg_p"].append(s_gla)
        outs["sr_p"].append(s_ret)
        outs["mk_p"].append(mkv4[0, :, :, :MEM_W].reshape(bp, n_mem, MEM_HEADS, MEM_DH))
        outs["mv_p"].append(mkv4[0, :, :, MEM_W:].reshape(bp, n_mem, MEM_HEADS, MEM_DH))

        zs = _matmul(xs, w_in_p, w_rows=True, g=g_mix[l], name="in_proj_s")
        zs3 = zs[:bs].reshape(bs, 1, N_PACK)
        ff_col = zs[:bs, OFF_SM + SM_FF:OFF_SM + SM_FF + FOX_HEADS].reshape(bs, FOX_HEADS, 1)
        heads = lambda off: zs[:bs, off:off + FOX_W].reshape(bs, FOX_HEADS, FOX_DH)
        o_fox_s, lf_s = _fox_decode(page_table, heads(OFF_FQ), heads(OFF_FK), heads(OFF_FV), ff_col,
                                    b_fox_f[l].reshape(FOX_HEADS, 1), k_pool, v_pool, lf_pool, sf_pool, l)
        o_mem_s = _mem_attn(zs3, mem_k_s, mem_v_s, layer=l, kcol=0, vcol=0, out_dtype=F32)
        o_gla_s, sg_s = _gla_step(zs3, w2p, b_gla_a[l], g_gla[l], state_gla, l)
        o_ret_s, sr_s = _ret_step(zs3, cos_s, sin_s, lg_tab, g_ret[l], state_ret, l)
        xs = _dense_tail(xs, [to_bf16_rows(a) for a in (o_fox_s, o_gla_s, o_ret_s, o_mem_s)], zs, wb, l)
        outs["fk_s"].append(zs3[:, :, OFF_FK:OFF_FK + FOX_W].reshape(bs, 1, FOX_HEADS, FOX_DH))
        outs["fv_s"].append(zs3[:, :, OFF_FV:OFF_FV + FOX_W].reshape(bs, 1, FOX_HEADS, FOX_DH))
        outs["fl_s"].append(lf_s.reshape(bs, 1, FOX_HEADS))
        outs["sg_s"].append(sg_s)
        outs["sr_s"].append(sr_s)

    y_prompt = _rmsnorm(xp, g_final).reshape(bp, t, d)
    y_sample = _rmsnorm(xs, g_final)[:bs].reshape(bs, 1, d)
    st = lambda k: jnp.stack(outs[k])
    fk_p = fk_stack.reshape(depth, bp, t, FOX_HEADS, FOX_DH)
    fv_p = fv_stack.reshape(depth, bp, t, FOX_HEADS, FOX_DH)
    return (y_prompt, y_sample, fk_p, fv_p, st("fl_p"), st("sg_p"), st("sr_p"), st("mk_p"), st("mv_p"),
            st("fk_s"), st("fv_s"), st("fl_s"), st("sg_s"), st("sr_s"))
```

```python
import functools

import numpy as np
import jax
import jax.numpy as jnp
from jax import lax
from jax.experimental import pallas as pl
from jax.experimental.pallas import tpu as pltpu

F32 = jnp.float32
BF16 = jnp.bfloat16

D_MODEL = 2048
FOX_HEADS = 8
FOX_DH = D_MODEL // 16
FOX_W = FOX_HEADS * FOX_DH
GLA_HEADS = 4
GLA_DK = D_MODEL // 16
GLA_DV = D_MODEL // 8
GLA_KW = GLA_HEADS * GLA_DK
GLA_VW = GLA_HEADS * GLA_DV
GLA_RANK = 16
GLA_TAU = 16.0
RET_HEADS = 4
RET_DH = D_MODEL // 8
RET_W = RET_HEADS * RET_DH
MEM_HEADS = 4
MEM_DH = D_MODEL // 8
MEM_W = MEM_HEADS * MEM_DH
N_BRANCH = 4
BRANCH_W = D_MODEL // 2
D_FF = 4 * D_MODEL
PAGE_SIZE = 128
ROPE_BASE = 10000.0
EPS = 1e-6
LOG2E = 1.4426950408889634

LANES = 128
SUBLANES = 8
BF16_ROWS = 16
VMEM_LIMIT_BYTES = 56 * 1024 * 1024

OFF_FQ = 0
OFF_FK = OFF_FQ + FOX_W
OFF_FV = OFF_FK + FOX_W
OFF_GQ = OFF_FV + FOX_W
OFF_GK = OFF_GQ + GLA_KW
OFF_GV = OFF_GK + GLA_KW
OFF_GR = OFF_GV + GLA_VW
OFF_RQ = OFF_GR + GLA_VW
OFF_RK = OFF_RQ + RET_W
OFF_RV = OFF_RK + RET_W
OFF_RG = OFF_RV + RET_W
OFF_MQ = OFF_RG + RET_W
OFF_GT = OFF_MQ + MEM_W
OFF_SM = OFF_GT + N_BRANCH * D_MODEL
SM_FF = 0
SM_GA = FOX_HEADS
N_PACK = OFF_SM + LANES

GLA_CHUNK = 64
GLA_SUB = 8
GLA_SUB_LOG2 = 3
GLA_CHUNKS_PER_STEP = 4
FOX_PAGES_PER_STEP = 8
FOX_DH_LOG2 = 7
assert GLA_SUB == 1 << GLA_SUB_LOG2 and FOX_DH == 1 << FOX_DH_LOG2 and LANES & (LANES - 1) == 0
RET_CHUNK = 256
REPACK_ROWS = 1024


def _cparams(sem):
    return pltpu.CompilerParams(dimension_semantics=sem, vmem_limit_bytes=VMEM_LIMIT_BYTES)


def _log_sigmoid(x):
    return jnp.minimum(x, 0.0) - jnp.log1p(jnp.exp(-jnp.abs(x)))


def _sigmoid(x):
    return 1.0 / (1.0 + jnp.exp(-x))


def _dot(a, b):
    return jnp.dot(a, b, preferred_element_type=F32)


def _dot_nt(a, b):
    return lax.dot_general(a, b, (((1,), (1,)), ((), ())), preferred_element_type=F32)


def _dot_tn(a, b):
    return lax.dot_general(a, b, (((0,), (0,)), ((), ())), preferred_element_type=F32)


def _dot01(m01, x):
    hi = x.astype(BF16)
    r1 = x - hi.astype(F32)
    mid = r1.astype(BF16)
    lo = (r1 - mid.astype(F32)).astype(BF16)
    return _dot(m01, hi) + _dot(m01, mid) + _dot(m01, lo)


def _row_to_cols(row):
    return jnp.broadcast_to(row, (LANES, LANES)).T


def _head_norm_gate(o, gain, gate):
    mu = jnp.mean(o, axis=1, keepdims=True)
    d = o - mu
    var = jnp.mean(d * d, axis=1, keepdims=True)
    y = d * lax.rsqrt(var + EPS) * gain
    return y * (gate * _sigmoid(gate))


def _mm_body(*refs, norm, act, has_res, stage, bf16_copy, w_rows):
    it = iter(refs)
    x_ref = next(it)
    g_ref = next(it) if norm else None
    w_ref = next(it)
    r_ref = next(it) if has_res else None
    o_ref = next(it)
    ob_ref = next(it) if bf16_copy else None
    h_ref = next(it) if stage else None
    if stage:
        @pl.when(pl.program_id(1) == 0)
        def _():
            x = x_ref[...].astype(F32)
            if norm:
                ms = jnp.mean(x * x, axis=-1, keepdims=True)
                x = x * lax.rsqrt(ms + EPS) * g_ref[...]
            h_ref[...] = x.astype(BF16)

        h = h_ref[...]
    else:
        h = x_ref[...]
    acc = _dot_nt(h, w_ref[...]) if w_rows else _dot(h, w_ref[...])
    if act == "relu2":
        acc = jnp.square(jnp.maximum(acc, 0.0))
    if has_res:
        acc = acc + r_ref[...]
    o_ref[...] = acc.astype(o_ref.dtype)
    if bf16_copy:
        ob_ref[...] = acc.astype(BF16)


def _pick_tile(n, candidates):
    for c in candidates:
        if n % c == 0:
            return c
    return n


def _matmul(x, w, *, layer=None, w_rows=False, g=None, res=None, act=None, out_dtype=F32, bf16_copy=False,
            name="proj"):
    m, k = x.shape
    n = w.shape[0] if w_rows else w.shape[-1]
    norm = g is not None
    stage = norm or x.dtype != BF16
    if m >= 1024:
        tm = 1024 if k <= 2048 else 512
    else:
        tm = m
    tn = _pick_tile(n, (1152, 1024, 512))
    if k > 2048:
        tn = min(tn, 512)
    assert m % tm == 0 and n % tn == 0 and tm % BF16_ROWS == 0
    in_specs = [pl.BlockSpec((tm, k), lambda i, j: (i, 0))]
    args = [x]
    if norm:
        in_specs.append(pl.BlockSpec((1, k), lambda i, j: (0, 0)))
        args.append(g.reshape(1, k).astype(F32))
    if w_rows:
        in_specs.append(pl.BlockSpec((tn, k), lambda i, j: (j, 0)))
    elif layer is None:
        in_specs.append(pl.BlockSpec((k, tn), lambda i, j: (0, j)))
    else:
        in_specs.append(pl.BlockSpec((None, k, tn), lambda i, j: (layer, 0, j)))
    args.append(w)
    if res is not None:
        in_specs.append(pl.BlockSpec((tm, tn), lambda i, j: (i, j)))
        args.append(res)
    scratch = [pltpu.VMEM((tm, k), BF16)] if stage else []
    o_spec = pl.BlockSpec((tm, tn), lambda i, j: (i, j))
    o_shape = jax.ShapeDtypeStruct((m, n), out_dtype)
    return pl.pallas_call(
        functools.partial(_mm_body, norm=norm, act=act, has_res=res is not None, stage=stage, bf16_copy=bf16_copy,
                          w_rows=w_rows),
        grid=(m // tm, n // tn),
        in_specs=in_specs,
        out_specs=[o_spec, o_spec] if bf16_copy else o_spec,
        out_shape=[o_shape, jax.ShapeDtypeStruct((m, n), BF16)] if bf16_copy else o_shape,
        scratch_shapes=scratch,
        compiler_params=_cparams(("parallel", "arbitrary")),
        name=name,
    )(*args)


def _rmsnorm_body(x_ref, g_ref, o_ref):
    x = x_ref[...]
    ms = jnp.mean(x * x, axis=-1, keepdims=True)
    o_ref[...] = x * lax.rsqrt(ms + EPS) * g_ref[...]


def _rmsnorm(x, g):
    m, k = x.shape
    tm = _pick_tile(m, (512,))
    return pl.pallas_call(
        _rmsnorm_body,
        grid=(m // tm,),
        in_specs=[pl.BlockSpec((tm, k), lambda i: (i, 0)), pl.BlockSpec((1, k), lambda i: (0, 0))],
        out_specs=pl.BlockSpec((tm, k), lambda i: (i, 0)),
        out_shape=jax.ShapeDtypeStruct((m, k), F32),
        compiler_params=_cparams(("parallel",)),
        name="final_norm",
    )(x, g.reshape(1, k))


def _fox_gate_body(x_ref, b_ref, lf_ref, c_ref):
    lf = _log_sigmoid(x_ref[...] + b_ref[...])
    lf_ref[...] = lf
    t = lf.shape[1]
    lane = lax.broadcasted_iota(jnp.int32, lf.shape, 1)
    c = lf
    shift = 1
    while shift < t:
        c = c + jnp.where(lane >= shift, pltpu.roll(c, shift, axis=1), 0.0)
        shift *= 2
    c_ref[...] = c


def _fox_gate(ff_t, bias):
    b, h, t = ff_t.shape
    spec = pl.BlockSpec((None, h, t), lambda i: (i, 0, 0))
    return pl.pallas_call(
        _fox_gate_body,
        grid=(b,),
        in_specs=[spec, pl.BlockSpec((h, 1), lambda i: (0, 0))],
        out_specs=[spec, spec],
        out_shape=[jax.ShapeDtypeStruct((b, h, t), F32)] * 2,
        compiler_params=_cparams(("parallel",)),
        name="fox_gate",
    )(ff_t, bias.reshape(h, 1))


def _fox_attn_body(q_ref, k_ref, v_ref, ck_ref, o_ref, q_sc, m_sc, l_sc, acc_sc, *, tile):
    qi = pl.program_id(1)
    ki = pl.program_id(2)
    nk = pl.num_programs(2)

    @pl.when(ki == 0)
    def _():
        q_sc[...] = (q_ref[...] * (FOX_DH ** -0.5 * LOG2E)).astype(BF16)
        m_sc[...] = jnp.full(m_sc.shape, -jnp.inf, F32)
        l_sc[...] = jnp.zeros(l_sc.shape, F32)
        acc_sc[...] = jnp.zeros(acc_sc.shape, F32)

    def step(on_diagonal):
        ck = ck_ref[...] * LOG2E
        ones = jnp.ones((tile, FOX_DH), BF16)
        if on_diagonal:
            row = lax.broadcasted_iota(jnp.int32, (tile, tile), 0)
            col = lax.broadcasted_iota(jnp.int32, (tile, tile), 1)
            visible = col <= row
        for h in range(FOX_HEADS):
            sl = slice(h * FOX_DH, (h + 1) * FOX_DH)
            s = _dot_nt(q_sc[:, sl], k_ref[:, sl]) - ck[h:h + 1, :]
            if on_diagonal:
                s = jnp.where(visible, s, -jnp.inf)
            m_prev = m_sc[h]
            m_new = jnp.maximum(m_prev, jnp.max(s, axis=1, keepdims=True))
            alpha = jnp.exp2(m_prev - m_new)
            p = jnp.exp2(s - jnp.concatenate([m_new] * (tile // FOX_DH), axis=1)).astype(BF16)
            pv = _dot(p, jnp.concatenate([v_ref[:, sl], ones], axis=1))
            l_sc[h] = alpha * l_sc[h] + pv[:, FOX_DH:]
            acc_sc[:, sl] = alpha * acc_sc[:, sl] + pv[:, :FOX_DH]
            m_sc[h] = m_new

    pl.when(ki < qi)(functools.partial(step, False))
    pl.when(ki == qi)(functools.partial(step, True))

    @pl.when(ki == nk - 1)
    def _():
        for h in range(FOX_HEADS):
            sl = slice(h * FOX_DH, (h + 1) * FOX_DH)
            o_ref[:, sl] = (acc_sc[:, sl] / l_sc[h]).astype(o_ref.dtype)


def _fox_attn(z3, zb3, c_row):
    b, t, _ = z3.shape
    tile = _pick_tile(t, (512, 256, 128))
    n = t // tile
    qcol, kcol, vcol = OFF_FQ // FOX_W, OFF_FK // FOX_W, OFF_FV // FOX_W
    return pl.pallas_call(
        functools.partial(_fox_attn_body, tile=tile),
        grid=(b, n, n),
        in_specs=[
            pl.BlockSpec((None, tile, FOX_W), lambda bi, qi, ki: (bi, qi, qcol)),
            pl.BlockSpec((None, tile, FOX_W), lambda bi, qi, ki: (bi, jnp.minimum(ki, qi), kcol)),
            pl.BlockSpec((None, tile, FOX_W), lambda bi, qi, ki: (bi, jnp.minimum(ki, qi), vcol)),
            pl.BlockSpec((None, FOX_HEADS, tile), lambda bi, qi, ki: (bi, 0, jnp.minimum(ki, qi))),
        ],
        out_specs=pl.BlockSpec((None, tile, FOX_W), lambda bi, qi, ki: (bi, qi, 0)),
        out_shape=jax.ShapeDtypeStruct((b, t, FOX_W), BF16),
        scratch_shapes=[
            pltpu.VMEM((tile, FOX_W), BF16),
            pltpu.VMEM((FOX_HEADS, tile, FOX_DH), F32),
            pltpu.VMEM((FOX_HEADS, tile, FOX_DH), F32),
            pltpu.VMEM((tile, FOX_W), F32),
        ],
        compiler_params=_cparams(("parallel", "parallel", "arbitrary")),
        name="fox_attn",
    )(z3, zb3, zb3, c_row)


def _mem_attn_body(q_ref, k_ref, v_ref, o_ref, *, rows):
    scale = MEM_DH ** -0.5
    for h in range(MEM_HEADS):
        sl = slice(h * MEM_DH, (h + 1) * MEM_DH)
        q = q_ref[:, sl]
        if rows < SUBLANES:
            q = jnp.broadcast_to(q, (SUBLANES, MEM_DH))
        s = _dot_nt(q.astype(BF16), k_ref[:, sl].astype(BF16)) * scale
        m = jnp.max(s, axis=1, keepdims=True)
        p = jnp.exp(s - m)
        p = p / jnp.sum(p, axis=1, keepdims=True)
        o = _dot(p.astype(BF16), v_ref[:, sl].astype(BF16))
        o_ref[:, sl] = o[:rows].astype(o_ref.dtype)


def _mem_attn(z3, k4, v4, *, layer, kcol, vcol, out_dtype):
    b, t, _ = z3.shape
    n_mem = k4.shape[2]
    tq = _pick_tile(t, (512, 256, 128))
    qcol = OFF_MQ // MEM_W
    return pl.pallas_call(
        functools.partial(_mem_attn_body, rows=tq),
        grid=(b, t // tq),
        in_specs=[
            pl.BlockSpec((None, tq, MEM_W), lambda bi, qi: (bi, qi, qcol)),
            pl.BlockSpec((None, None, n_mem, MEM_W), lambda bi, qi: (layer, bi, 0, kcol)),
            pl.BlockSpec((None, None, n_mem, MEM_W), lambda bi, qi: (layer, bi, 0, vcol)),
        ],
        out_specs=pl.BlockSpec((None, tq, MEM_W), lambda bi, qi: (bi, qi, 0)),
        out_shape=jax.ShapeDtypeStruct((b, t, MEM_W), out_dtype),
        compiler_params=_cparams(("parallel", "parallel")),
        name="mem_attn",
    )(z3, k4, v4)


def _rotary(x, cos, sin):
    half = x.shape[1] // 2
    x1, x2 = x[:, :half], x[:, half:]
    return jnp.concatenate([x1 * cos - x2 * sin, x1 * sin + x2 * cos], axis=1)


def _ret_body(q_ref, k_ref, v_ref, g_ref, cos_ref, sin_ref, lg_ref, gn_ref, o_ref, s_ref, st_sc, *, chunk):
    c = pl.program_id(1)
    nc = pl.num_programs(1)

    @pl.when(c == 0)
    def _():
        st_sc[...] = jnp.zeros(st_sc.shape, F32)

    cos, sin = cos_ref[...], sin_ref[...]
    row = lax.broadcasted_iota(jnp.int32, (chunk, chunk), 0)
    col = lax.broadcasted_iota(jnp.int32, (chunk, chunk), 1)
    lag = (row - col).astype(F32)
    t = lax.broadcasted_iota(jnp.int32, (chunk, 1), 0).astype(F32)
    for h in range(RET_HEADS):
        sl = slice(h * RET_DH, (h + 1) * RET_DH)
        lg = lg_ref[h][:, :1]
        q = _rotary(q_ref[:, sl], cos, sin)
        k = _rotary(k_ref[:, sl], cos, sin) * (RET_DH ** -0.5)
        v = v_ref[:, sl].astype(BF16)
        decay = jnp.where(col <= row, jnp.exp(lag * lg), 0.0)
        a = _dot_nt(q.astype(BF16), k.astype(BF16)) * decay
        s0 = st_sc[h]
        o = _dot(a.astype(BF16), v) + jnp.exp((t + 1.0) * lg) * _dot(q.astype(BF16), s0.astype(BF16))
        k_end = k * jnp.exp((chunk - 1.0 - t) * lg)
        s_new = jnp.exp(chunk * lg) * s0 + _dot_tn(k_end.astype(BF16), v)
        st_sc[h] = s_new
        o_ref[:, sl] = _head_norm_gate(o, gn_ref[:, sl], g_ref[:, sl]).astype(o_ref.dtype)

        @pl.when(c == nc - 1)
        def _():
            s_ref[h] = s_new


def _ret_prompt(z3, cos, sin, lg_tab, g_ret):
    b, t, _ = z3.shape
    chunk = _pick_tile(t, (RET_CHUNK, 128, 64))
    zspec = lambda off: pl.BlockSpec((None, chunk, RET_W), lambda bi, c: (bi, c, off // RET_W))
    return pl.pallas_call(
        functools.partial(_ret_body, chunk=chunk),
        grid=(b, t // chunk),
        in_specs=[
            zspec(OFF_RQ), zspec(OFF_RK), zspec(OFF_RV), zspec(OFF_RG),
            pl.BlockSpec((chunk, RET_DH // 2), lambda bi, c: (c, 0)),
            pl.BlockSpec((chunk, RET_DH // 2), lambda bi, c: (c, 0)),
            pl.BlockSpec((RET_HEADS, 1, LANES), lambda bi, c: (0, 0, 0)),
            pl.BlockSpec((1, RET_W), lambda bi, c: (0, 0)),
        ],
        out_specs=[
            pl.BlockSpec((None, chunk, RET_W), lambda bi, c: (bi, c, 0)),
            pl.BlockSpec((None, RET_HEADS, RET_DH, RET_DH), lambda bi, c: (bi, 0, 0, 0)),
        ],
        out_shape=[
            jax.ShapeDtypeStruct((b, t, RET_W), BF16),
            jax.ShapeDtypeStruct((b, RET_HEADS, RET_DH, RET_DH), F32),
        ],
        scratch_shapes=[pltpu.VMEM((RET_HEADS, RET_DH, RET_DH), F32)],
        compiler_params=_cparams(("parallel", "arbitrary")),
        name="ret_prompt",
    )(z3, z3, z3, z3, cos, sin, lg_tab, g_ret.reshape(1, RET_W))


def _gla_loga(zs, w2_ref, ba_ref):
    pre = _dot(zs.astype(BF16), w2_ref[...]) + ba_ref[...]
    return _log_sigmoid(pre) * (1.0 / GLA_TAU)


def _gla_chunk(q, k, v_ref, gr_ref, gn_ref, o_ref, loga, st, tri, hsum, rows):
    chunk = q.shape[0]
    n_sub = chunk // GLA_SUB
    b = _dot01(tri, loga)
    row = lax.broadcasted_iota(jnp.int32, (chunk, GLA_KW), 0)
    b_last = b[chunk - 1:chunk, :]
    beta = jnp.zeros_like(b)
    for i in range(1, n_sub):
        beta = jnp.where(row >= i * GLA_SUB, b[i * GLA_SUB - 1:i * GLA_SUB, :], beta)
    q_rel = q * jnp.exp(b - beta)
    q_abs = q * jnp.exp(b)
    k_end = k * jnp.exp(b_last - b)
    k_rel = []
    for i in range(1, n_sub):
        bi = b[i * GLA_SUB - 1:i * GLA_SUB, :]
        k_rel.append(jnp.where(row < i * GLA_SUB, k * jnp.exp(jnp.minimum(bi - b, 0.0)), 0.0).astype(BF16))

    prods = []
    for delta in range(GLA_SUB):
        kd = k if delta == 0 else pltpu.roll(k, delta, axis=0)
        bd = b if delta == 0 else pltpu.roll(b, delta, axis=0)
        prods.append((q * kd * jnp.exp(jnp.minimum(b - bd, 0.0))).astype(BF16))
    sums = _dot(jnp.concatenate(prods, axis=0), hsum)
    rowb = lax.broadcasted_iota(jnp.int32, (chunk, GLA_HEADS * LANES), 0)
    colb = lax.broadcasted_iota(jnp.int32, (chunk, GLA_HEADS * LANES), 1) & (LANES - 1)
    rsub = rowb & (GLA_SUB - 1)
    band = jnp.zeros((chunk, GLA_HEADS * LANES), F32)
    for delta in range(GLA_SUB):
        keep = jnp.logical_and(colb == rowb - delta, rsub >= delta)
        band = band + jnp.where(keep, sums[delta * chunk:(delta + 1) * chunk], 0.0)

    sub_of_row = row[:, :GLA_DK] >> GLA_SUB_LOG2
    new_st = []
    for h in range(GLA_HEADS):
        ks = slice(h * GLA_DK, (h + 1) * GLA_DK)
        vs = slice(h * GLA_DV, (h + 1) * GLA_DV)
        v = v_ref[rows, vs].astype(BF16)
        lhs = jnp.concatenate(
            [jnp.where(sub_of_row == i, q_rel[:, ks], 0.0) for i in range(1, n_sub)], axis=1).astype(BF16)
        rhs = jnp.concatenate([kr[:, ks] for kr in k_rel], axis=1)
        a = _dot_nt(lhs, rhs) + band[:, h * LANES:h * LANES + chunk]
        o = _dot(a.astype(BF16), v) + _dot_nt(q_abs[:, ks].astype(BF16), st[h].astype(BF16))
        new_st.append(st[h] * jnp.exp(b_last[:, ks]) + _dot_tn(v, k_end[:, ks].astype(BF16)))
        o_ref[rows, vs] = _head_norm_gate(o, gn_ref[:, vs], gr_ref[rows, vs]).astype(o_ref.dtype)
    return new_st


def _gla_body(q_ref, k_ref, v_ref, gr_ref, zs_ref, w2_ref, ba_ref, gn_ref, tri_ref, hsum_ref,
              o_ref, s_ref, st_sc, *, chunk, n_chunks):
    c = pl.program_id(1)
    nc = pl.num_programs(1)

    @pl.when(c == 0)
    def _():
        st_sc[...] = jnp.zeros(st_sc.shape, F32)

    loga = _gla_loga(zs_ref[...], w2_ref, ba_ref)
    tri, hsum = tri_ref[...], hsum_ref[...]
    st = [st_sc[h] for h in range(GLA_HEADS)]
    for ci in range(n_chunks):
        rows = slice(ci * chunk, (ci + 1) * chunk)
        q = q_ref[rows, :] * (GLA_DK ** -0.5)
        st = _gla_chunk(q, k_ref[rows, :], v_ref, gr_ref, gn_ref, o_ref, loga[rows], st, tri, hsum, rows)
    for h in range(GLA_HEADS):
        st_sc[h] = st[h]

    @pl.when(c == nc - 1)
    def _():
        for h in range(GLA_HEADS):
            s_ref[h] = st[h].T


def _gla_consts(chunk):
    r = np.arange(chunk)
    tri = (r[None, :] <= r[:, None]).astype(np.float32)
    lane = np.arange(GLA_KW)
    out = np.arange(GLA_HEADS * LANES)
    hsum = (lane[:, None] // GLA_DK == out[None, :] // LANES).astype(np.float32)
    return jnp.asarray(tri, BF16), jnp.asarray(hsum, BF16)


def _gla_prompt(z3, w2p, b_a, g_gla):
    b, t, _ = z3.shape
    chunk = GLA_CHUNK
    n_chunks = _pick_tile(t // chunk, (GLA_CHUNKS_PER_STEP, 2, 1))
    rows = chunk * n_chunks
    assert t % rows == 0
    tri, hsum = _gla_consts(chunk)
    full = lambda shape: pl.BlockSpec(shape, lambda bi, c: (0,) * len(shape))
    return pl.pallas_call(
        functools.partial(_gla_body, chunk=chunk, n_chunks=n_chunks),
        grid=(b, t // rows),
        in_specs=[
            pl.BlockSpec((None, rows, GLA_KW), lambda bi, c: (bi, c, OFF_GQ // GLA_KW)),
            pl.BlockSpec((None, rows, GLA_KW), lambda bi, c: (bi, c, OFF_GK // GLA_KW)),
            pl.BlockSpec((None, rows, GLA_VW), lambda bi, c: (bi, c, OFF_GV // GLA_VW)),
            pl.BlockSpec((None, rows, GLA_VW), lambda bi, c: (bi, c, OFF_GR // GLA_VW)),
            pl.BlockSpec((None, rows, LANES), lambda bi, c: (bi, c, OFF_SM // LANES)),
            full((LANES, GLA_KW)), full((1, GLA_KW)), full((1, GLA_VW)),
            full((chunk, chunk)), full((GLA_KW, GLA_HEADS * LANES)),
        ],
        out_specs=[
            pl.BlockSpec((None, rows, GLA_VW), lambda bi, c: (bi, c, 0)),
            pl.BlockSpec((None, GLA_HEADS, GLA_DK, GLA_DV), lambda bi, c: (bi, 0, 0, 0)),
        ],
        out_shape=[
            jax.ShapeDtypeStruct((b, t, GLA_VW), BF16),
            jax.ShapeDtypeStruct((b, GLA_HEADS, GLA_DK, GLA_DV), F32),
        ],
        scratch_shapes=[pltpu.VMEM((GLA_HEADS, GLA_DV, GLA_DK), F32)],
        compiler_params=_cparams(("parallel", "arbitrary")),
        name="gla_prompt",
    )(z3, z3, z3, z3, z3, w2p, b_a.reshape(1, GLA_KW), g_gla.reshape(1, GLA_VW), tri, hsum)


def _gla_step_body(q_ref, k_ref, v_ref, gr_ref, zs_ref, w2_ref, ba_ref, gn_ref, s0_ref, o_ref, s_ref):
    zs = jnp.broadcast_to(zs_ref[...], (BF16_ROWS, LANES))
    loga = _gla_loga(zs, w2_ref, ba_ref)[:1]
    q = q_ref[...] * (GLA_DK ** -0.5)
    k = k_ref[...]
    for h in range(GLA_HEADS):
        ks = slice(h * GLA_DK, (h + 1) * GLA_DK)
        vs = slice(h * GLA_DV, (h + 1) * GLA_DV)
        widen = lambda m: jnp.concatenate([m] * (GLA_DV // LANES), axis=1)
        a_c = widen(_row_to_cols(jnp.exp(loga[:, ks])))
        k_c = widen(_row_to_cols(k[:, ks]))
        q_c = widen(_row_to_cols(q[:, ks]))
        s_new = a_c * s0_ref[h] + k_c * v_ref[:, vs]
        s_ref[h] = s_new
        o = jnp.sum(q_c * s_new, axis=0, keepdims=True)
        o_ref[:, vs] = _head_norm_gate(o, gn_ref[:, vs], gr_ref[:, vs])


def _gla_step(zs3, w2p, b_a, g_gla, state, layer):
    b = zs3.shape[0]
    full = lambda shape: pl.BlockSpec(shape, lambda bi: (0,) * len(shape))
    return pl.pallas_call(
        _gla_step_body,
        grid=(b,),
        in_specs=[
            pl.BlockSpec((None, 1, GLA_KW), lambda bi: (bi, 0, OFF_GQ // GLA_KW)),
            pl.BlockSpec((None, 1, GLA_KW), lambda bi: (bi, 0, OFF_GK // GLA_KW)),
            pl.BlockSpec((None, 1, GLA_VW), lambda bi: (bi, 0, OFF_GV // GLA_VW)),
            pl.BlockSpec((None, 1, GLA_VW), lambda bi: (bi, 0, OFF_GR // GLA_VW)),
            pl.BlockSpec((None, 1, LANES), lambda bi: (bi, 0, OFF_SM // LANES)),
            full((LANES, GLA_KW)), full((1, GLA_KW)), full((1, GLA_VW)),
            pl.BlockSpec((None, None, GLA_HEADS, GLA_DK, GLA_DV), lambda bi: (layer, bi, 0, 0, 0)),
        ],
        out_specs=[
            pl.BlockSpec((None, 1, GLA_VW), lambda bi: (bi, 0, 0)),
            pl.BlockSpec((None, GLA_HEADS, GLA_DK, GLA_DV), lambda bi: (bi, 0, 0, 0)),
        ],
        out_shape=[
            jax.ShapeDtypeStruct((b, 1, GLA_VW), F32),
            jax.ShapeDtypeStruct((b, GLA_HEADS, GLA_DK, GLA_DV), F32),
        ],
        compiler_params=_cparams(("parallel",)),
        name="gla_step",
    )(zs3, zs3, zs3, zs3, zs3, w2p, b_a.reshape(1, GLA_KW), g_gla.reshape(1, GLA_VW), state)


def _ret_step_body(q_ref, k_ref, v_ref, g_ref, cos_ref, sin_ref, lg_ref, gn_ref, s0_ref, o_ref, s_ref):
    cos, sin = cos_ref[...], sin_ref[...]
    half = RET_DH // 2
    for h in range(RET_HEADS):
        sl = slice(h * RET_DH, (h + 1) * RET_DH)
        q = _rotary(q_ref[:, sl], cos, sin)
        k = _rotary(k_ref[:, sl], cos, sin) * (RET_DH ** -0.5)
        cols = lambda r: jnp.concatenate(
            [jnp.concatenate([_row_to_cols(r[:, i * half:(i + 1) * half])] * 2, axis=1) for i in range(2)], axis=0)
        gamma = jnp.exp(lg_ref[h][:, :1])
        s_new = gamma * s0_ref[h] + cols(k) * v_ref[:, sl]
        s_ref[h] = s_new
        o = jnp.sum(cols(q) * s_new, axis=0, keepdims=True)
        o_ref[:, sl] = _head_norm_gate(o, gn_ref[:, sl], g_ref[:, sl])


def _ret_step(zs3, cos, sin, lg_tab, g_ret, state, layer):
    b = zs3.shape[0]
    full = lambda shape: pl.BlockSpec(shape, lambda bi: (0,) * len(shape))
    zspec = lambda off: pl.BlockSpec((None, 1, RET_W), lambda bi: (bi, 0, off // RET_W))
    return pl.pallas_call(
        _ret_step_body,
        grid=(b,),
        in_specs=[
            zspec(OFF_RQ), zspec(OFF_RK), zspec(OFF_RV), zspec(OFF_RG),
            full((1, RET_DH // 2)), full((1, RET_DH // 2)), full((RET_HEADS, 1, LANES)), full((1, RET_W)),
            pl.BlockSpec((None, None, RET_HEADS, RET_DH, RET_DH), lambda bi: (layer, bi, 0, 0, 0)),
        ],
        out_specs=[
            pl.BlockSpec((None, 1, RET_W), lambda bi: (bi, 0, 0)),
            pl.BlockSpec((None, RET_HEADS, RET_DH, RET_DH), lambda bi: (bi, 0, 0, 0)),
        ],
        out_shape=[
            jax.ShapeDtypeStruct((b, 1, RET_W), F32),
            jax.ShapeDtypeStruct((b, RET_HEADS, RET_DH, RET_DH), F32),
        ],
        compiler_params=_cparams(("parallel",)),
        name="ret_step",
    )(zs3, zs3, zs3, zs3, cos, sin, lg_tab, g_ret.reshape(1, RET_W), state)


def _page_suffix_body(x_ref, o_ref):
    x = x_ref[...]
    width = x.shape[1]
    lane = lax.broadcasted_iota(jnp.int32, x.shape, 1)
    incl = x
    shift = FOX_HEADS
    while shift < width:
        moved = pltpu.roll(incl, width - shift, axis=1)
        incl = incl + jnp.where(lane < width - shift, moved, 0.0)
        shift *= 2
    o_ref[...] = incl


def _page_suffix(lf_pages):
    n, width = lf_pages.shape
    rows = _pick_tile(n, (64, 32, 16, 8))
    spec = pl.BlockSpec((rows, width), lambda i: (i, 0))
    return pl.pallas_call(
        _page_suffix_body,
        grid=(n // rows,),
        in_specs=[spec],
        out_specs=spec,
        out_shape=jax.ShapeDtypeStruct((n, width), F32),
        compiler_params=_cparams(("parallel",)),
        name="page_suffix",
    )(lf_pages)


def _fox_decode_body(*refs, pages):
    q_ref, kn_ref, vn_ref, ff_ref, bf_ref = refs[1:6]
    kp_refs = refs[6:6 + pages]
    vp_refs = refs[6 + pages:6 + 2 * pages]
    lp_refs = refs[6 + 2 * pages:6 + 3 * pages]
    sf_refs = refs[6 + 3 * pages:6 + 4 * pages]
    o_ref, lf_ref, m_sc, l_sc, tail_sc, acc_sc = refs[6 + 4 * pages:]
    g = pl.program_id(1)
    n_groups = pl.num_programs(1)
    scale = FOX_DH ** -0.5
    width = PAGE_SIZE * FOX_HEADS
    sub = lax.broadcasted_iota(jnp.int32, (FOX_HEADS, width), 0)
    lane = lax.broadcasted_iota(jnp.int32, (FOX_HEADS, width), 1)
    own = (lane & (FOX_HEADS - 1)) == sub
    q = q_ref[...]
    qb = q.astype(BF16)

    @pl.when(g == 0)
    def _():
        lf_new = _log_sigmoid(ff_ref[...] + bf_ref[...])
        lf_ref[...] = lf_new
        tail_sc[...] = jnp.broadcast_to(lf_new, tail_sc.shape)
        s_new = jnp.sum(q * kn_ref[...], axis=1, keepdims=True) * scale
        m_sc[...] = jnp.broadcast_to(s_new, m_sc.shape)
        l_sc[...] = jnp.ones(l_sc.shape, F32)
        acc_sc[...] = vn_ref[...]

    tail = tail_sc[:, :1]
    scores = []
    for i in range(pages):
        incl = jnp.broadcast_to(sf_refs[i][...], (FOX_HEADS, width))
        bias = tail + (incl - lp_refs[i][...])
        s = _dot_nt(qb, kp_refs[i][...].astype(BF16)) * scale + bias
        scores.append(jnp.where(own, s, -jnp.inf))
        tail = tail + jnp.sum(jnp.where(lane == sub, incl, 0.0), axis=1, keepdims=True)
    m_prev = m_sc[:, :1]
    m_new = m_prev
    for s in scores:
        m_new = jnp.maximum(m_new, jnp.max(s, axis=1, keepdims=True))
    alpha = jnp.exp(m_prev - m_new)
    l_new = alpha * l_sc[:, :1]
    acc = alpha * acc_sc[...]
    for i in range(pages):
        pr = jnp.exp(scores[i] - m_new)
        l_new = l_new + jnp.sum(pr, axis=1, keepdims=True)
        acc = acc + _dot(pr.astype(BF16), vp_refs[i][...].astype(BF16))
    acc_sc[...] = acc
    l_sc[...] = jnp.broadcast_to(l_new, l_sc.shape)
    m_sc[...] = jnp.broadcast_to(m_new, m_sc.shape)
    tail_sc[...] = jnp.broadcast_to(tail, tail_sc.shape)

    @pl.when(g == n_groups - 1)
    def _():
        o_ref[...] = acc / l_new


def _fox_decode(page_table, q8, kn8, vn8, ff_col, bf_col, k_pool, v_pool, lf_pool, sf_pool, layer):
    b, n_pages = page_table.shape
    width = PAGE_SIZE * FOX_HEADS
    pages = _pick_tile(n_pages, (FOX_PAGES_PER_STEP, 2, 1))

    def page(i):
        return lambda bi, g, pt: (layer, pt[bi, n_pages - 1 - (g * pages + i)], 0, 0)

    tok = pl.BlockSpec((None, FOX_HEADS, FOX_DH), lambda bi, g, pt: (bi, 0, 0))
    col = pl.BlockSpec((None, FOX_HEADS, 1), lambda bi, g, pt: (bi, 0, 0))
    kv_specs = [pl.BlockSpec((None, None, width, FOX_DH), page(i)) for i in range(pages)]
    row_specs = [pl.BlockSpec((None, None, 1, width), page(i)) for i in range(pages)]
    grid_spec = pltpu.PrefetchScalarGridSpec(
        num_scalar_prefetch=1,
        grid=(b, n_pages // pages),
        in_specs=[tok, tok, tok, col, pl.BlockSpec((FOX_HEADS, 1), lambda bi, g, pt: (0, 0))]
        + kv_specs + kv_specs + row_specs + row_specs,
        out_specs=[tok, col],
        scratch_shapes=[
            pltpu.VMEM((FOX_HEADS, LANES), F32),
            pltpu.VMEM((FOX_HEADS, LANES), F32),
            pltpu.VMEM((FOX_HEADS, LANES), F32),
            pltpu.VMEM((FOX_HEADS, FOX_DH), F32),
        ],
    )
    return pl.pallas_call(
        functools.partial(_fox_decode_body, pages=pages),
        grid_spec=grid_spec,
        out_shape=[
            jax.ShapeDtypeStruct((b, FOX_HEADS, FOX_DH), F32),
            jax.ShapeDtypeStruct((b, FOX_HEADS, 1), F32),
        ],
        compiler_params=_cparams(("parallel", "arbitrary")),
        name="fox_decode",
    )(page_table, q8, kn8, vn8, ff_col, bf_col, *([k_pool] * pages), *([v_pool] * pages),
      *([lf_pool] * pages), *([sf_pool] * pages))


def _merge_body(o0, o1, o2, o3, g0, g1, g2, g3, w_ref, out_ref):
    acc = None
    for i, (o_ref, g_ref) in enumerate(((o0, g0), (o1, g1), (o2, g2), (o3, g3))):
        y = (0.5 * jnp.tanh(0.5 * g_ref[...].astype(F32)) + 0.5) * _dot(o_ref[...], w_ref[i])
        acc = y if acc is None else acc + y
    out_ref[...] = acc.astype(out_ref.dtype)


def _merge(branches, z, w_branch, layer):
    m = z.shape[0]
    tm = _pick_tile(m, (512,))
    tn = 512
    assert D_MODEL % tn == 0 and OFF_GT % tn == 0
    o_spec = pl.BlockSpec((tm, BRANCH_W), lambda i, j: (i, 0))
    g_spec = lambda br: pl.BlockSpec((tm, tn), lambda i, j: (i, (OFF_GT + br * D_MODEL) // tn + j))
    return pl.pallas_call(
        _merge_body,
        grid=(m // tm, D_MODEL // tn),
        in_specs=[o_spec] * N_BRANCH + [g_spec(br) for br in range(N_BRANCH)]
        + [pl.BlockSpec((None, N_BRANCH, BRANCH_W, tn), lambda i, j: (layer, 0, 0, j))],
        out_specs=pl.BlockSpec((tm, tn), lambda i, j: (i, j)),
        out_shape=jax.ShapeDtypeStruct((m, D_MODEL), BF16),
        compiler_params=_cparams(("parallel", "arbitrary")),
        name="merge",
    )(*branches, z, z, z, z, w_branch)


def _heads_out_body(k_ref, v_ref, k_any, v_any, ko_ref, vo_ref):
    del k_any, v_any
    for h in range(FOX_HEADS):
        sl = slice(h * FOX_DH, (h + 1) * FOX_DH)
        ko_ref[:, h, :] = k_ref[:, sl]
        vo_ref[:, h, :] = v_ref[:, sl]


def _heads_out(z, k_stack, v_stack, layer):
    m = z.shape[0]
    tm = _pick_tile(m, (512,))
    o_spec = pl.BlockSpec((None, tm, FOX_HEADS, FOX_DH), lambda i: (layer, i, 0, 0))
    return pl.pallas_call(
        _heads_out_body,
        grid=(m // tm,),
        in_specs=[
            pl.BlockSpec((tm, FOX_W), lambda i: (i, OFF_FK // FOX_W)),
            pl.BlockSpec((tm, FOX_W), lambda i: (i, OFF_FV // FOX_W)),
            pl.BlockSpec(memory_space=pl.ANY),
            pl.BlockSpec(memory_space=pl.ANY),
        ],
        out_specs=[o_spec, o_spec],
        out_shape=[jax.ShapeDtypeStruct(k_stack.shape, F32), jax.ShapeDtypeStruct(v_stack.shape, F32)],
        input_output_aliases={2: 0, 3: 1},
        compiler_params=_cparams(("parallel",)),
        name="heads_out",
    )(z, z, k_stack, v_stack)


def _repack_body(src_ref, ff_ref, ga_ref, o_ref, *, n_big):
    j = pl.program_id(0)

    @pl.when(j < n_big)
    def _():
        o_ref[...] = src_ref[0].astype(BF16)

    @pl.when(j == n_big)
    def _():
        k = o_ref.shape[1]
        pad = jnp.zeros((o_ref.shape[0] - FOX_HEADS - GLA_RANK, k), F32)
        o_ref[...] = jnp.concatenate([ff_ref[0], ga_ref[0], pad], axis=0).astype(BF16)


def _pack_w_in(w_in_t, layer):
    _, d_in, k = w_in_t.shape
    c_ff = 3 * FOX_W
    c_ga = c_ff + FOX_HEADS + 2 * GLA_KW + 2 * GLA_VW
    n_big = OFF_SM // REPACK_ROWS
    assert OFF_GQ % REPACK_ROWS == 0 and OFF_RQ % REPACK_ROWS == 0 and OFF_SM % REPACK_ROWS == 0
    assert FOX_HEADS % SUBLANES == 0 and GLA_RANK % SUBLANES == 0 and SM_GA == SM_FF + FOX_HEADS

    def src_index(j):
        start = jnp.minimum(j, n_big - 1) * REPACK_ROWS
        shift = jnp.where(start < OFF_GQ, 0, jnp.where(start < OFF_RQ, FOX_HEADS, FOX_HEADS + GLA_RANK))
        return (layer, pl.multiple_of(start + shift, SUBLANES), 0)

    rows = lambda n, start: pl.BlockSpec((pl.Element(1), pl.Element(n), pl.Element(k)), lambda j: (layer, start, 0))
    return pl.pallas_call(
        functools.partial(_repack_body, n_big=n_big),
        grid=(n_big + 1,),
        in_specs=[
            pl.BlockSpec((pl.Element(1), pl.Element(REPACK_ROWS), pl.Element(k)), src_index),
            rows(FOX_HEADS, c_ff), rows(GLA_RANK, c_ga),
        ],
        out_specs=pl.BlockSpec((REPACK_ROWS, k), lambda j: (j, 0)),
        out_shape=jax.ShapeDtypeStruct((N_PACK, k), BF16),
        compiler_params=_cparams(("arbitrary",)),
        name="repack_w_in",
    )(w_in_t, w_in_t, w_in_t)


def _pack_w_gla_a2(w2):
    out = jnp.zeros((LANES, GLA_KW), F32).at[SM_GA:SM_GA + GLA_RANK].set(w2)
    return out.astype(BF16)


def _rope_tables(pos):
    half = RET_DH // 2
    inv = ROPE_BASE ** (-jnp.arange(half, dtype=F32) / half)
    ang = pos.astype(F32)[:, None] * inv[None, :]
    return jnp.cos(ang), jnp.sin(ang)


def _retention_log_decay_table():
    lg = jnp.log1p(-jnp.exp2(-5.0 - jnp.arange(RET_HEADS, dtype=F32)))
    return jnp.broadcast_to(lg[:, None, None], (RET_HEADS, 1, LANES))


def _dense_tail(x, branches, z, wb, layer):
    merged = _merge(branches, z, wb["w_branch"], layer)
    x = _matmul(merged, wb["w_out"], layer=layer, res=x, name="out_proj")
    u = _matmul(x, wb["w_ff1"], layer=layer, g=wb["g_mlp"][layer], act="relu2", out_dtype=BF16, name="ff1")
    return _matmul(u, wb["w_ff2"], layer=layer, res=x, name="ff2")


def kernel(x_prompt, x_sample, cache_fox_k, cache_fox_v, cache_fox_logf, state_gla, state_ret, cache_mem_k, cache_mem_v, page_table, mem_prompt, g_mix, w_in, b_fox_f, w_gla_a2, b_gla_a, g_gla, g_ret, w_mem_kv, w_branch, w_out, g_mlp, w_ff1, w_ff2, g_final):
    depth = w_in.shape[0]
    bp, t, d = x_prompt.shape
    bs, ts, _ = x_sample.shape
    assert ts == 1 and d == D_MODEL
    n_pool = cache_fox_k.shape[1]
    n_mem = mem_prompt.shape[1]
    past = page_table.shape[1] * PAGE_SIZE
    rows_s = -(-bs // BF16_ROWS) * BF16_ROWS

    cos_p, sin_p = _rope_tables(jnp.arange(t))
    cos_s, sin_s = _rope_tables(past + jnp.arange(ts))
    lg_tab = _retention_log_decay_table()
    k_pool = cache_fox_k.reshape(depth, n_pool, PAGE_SIZE * FOX_HEADS, FOX_DH)
    v_pool = cache_fox_v.reshape(depth, n_pool, PAGE_SIZE * FOX_HEADS, FOX_DH)
    lf_pages = cache_fox_logf.reshape(depth * n_pool, PAGE_SIZE * FOX_HEADS)
    lf_pool = lf_pages.reshape(depth, n_pool, 1, PAGE_SIZE * FOX_HEADS)
    sf_pool = _page_suffix(lf_pages).reshape(depth, n_pool, 1, PAGE_SIZE * FOX_HEADS)
    mem_k_s = cache_mem_k.reshape(depth, bs, n_mem, MEM_W)
    mem_v_s = cache_mem_v.reshape(depth, bs, n_mem, MEM_W)
    mem_prompt2 = mem_prompt.reshape(bp * n_mem, d)

    xp = x_prompt.reshape(bp * t, d)
    xs = jnp.zeros((rows_s, d), F32).at[:bs].set(x_sample.reshape(bs, d))
    fk_stack = jnp.zeros((depth, bp * t, FOX_HEADS, FOX_DH), F32)
    fv_stack = jnp.zeros((depth, bp * t, FOX_HEADS, FOX_DH), F32)
    outs = {k: [] for k in ("fl_p", "sg_p", "sr_p", "mk_p", "mv_p", "fk_s", "fv_s", "fl_s", "sg_s", "sr_s")}
    to_bf16_rows = lambda a: jnp.zeros((rows_s, a.size // bs), BF16).at[:bs].set(a.reshape(bs, -1).astype(BF16))

    wb = dict(w_branch=w_branch.astype(BF16), w_out=w_out.astype(BF16), g_mlp=g_mlp,
              w_ff1=w_ff1.astype(BF16), w_ff2=w_ff2.astype(BF16))
    w_mem_kv_b = w_mem_kv.astype(BF16)
    w_in_t = jnp.swapaxes(w_in, 1, 2)
    for l in range(depth):
        w_in_p = _pack_w_in(w_in_t, l)
        w2p = _pack_w_gla_a2(w_gla_a2[l])

        z, zb = _matmul(xp, w_in_p, w_rows=True, g=g_mix[l], bf16_copy=True, name="in_proj")
        z3 = z.reshape(bp, t, N_PACK)
        zb3 = zb.reshape(bp, t, N_PACK)
        ff_t = jnp.swapaxes(z3[:, :, OFF_SM + SM_FF:OFF_SM + SM_FF + FOX_HEADS], 1, 2)
        lf_t, c_t = _fox_gate(ff_t, b_fox_f[l])
        o_fox = _fox_attn(z3, zb3, c_t)
        mkv = _matmul(mem_prompt2, w_mem_kv_b, layer=l, name="mem_kv")
        mkv4 = mkv.reshape(1, bp, n_mem, 2 * MEM_W)
        o_mem = _mem_attn(z3, mkv4, mkv4, layer=0, kcol=0, vcol=1, out_dtype=BF16)
        o_gla, s_gla = _gla_prompt(z3, w2p, b_gla_a[l], g_gla[l])
        o_ret, s_ret = _ret_prompt(z3, cos_p, sin_p, lg_tab, g_ret[l])
        flat = lambda a: a.reshape(bp * t, -1)
        xp = _dense_tail(xp, [flat(o_fox), flat(o_gla), flat(o_ret), flat(o_mem)], zb, wb, l)
        fk_stack, fv_stack = _heads_out(z, fk_stack, fv_stack, l)
        outs["fl_p"].append(jnp.swapaxes(lf_t, 1, 2))
        outs["sg_p"].append(s_gla)
        outs["sr_p"].append(s_ret)
        outs["mk_p"].append(mkv4[0, :, :, :MEM_W].reshape(bp, n_mem, MEM_HEADS, MEM_DH))
        outs["mv_p"].append(mkv4[0, :, :, MEM_W:].reshape(bp, n_mem, MEM_HEADS, MEM_DH))

        zs = _matmul(xs, w_in_p, w_rows=True, g=g_mix[l], name="in_proj_s")
        zs3 = zs[:bs].reshape(bs, 1, N_PACK)
        ff_col = zs[:bs, OFF_SM + SM_FF:OFF_SM + SM_FF + FOX_HEADS].reshape(bs, FOX_HEADS, 1)
        heads = lambda off: zs[:bs, off:off + FOX_W].reshape(bs, FOX_HEADS, FOX_DH)
        o_fox_s, lf_s = _fox_decode(page_table, heads(OFF_FQ), heads(OFF_FK), heads(OFF_FV), ff_col,
                                    b_fox_f[l].reshape(FOX_HEADS, 1), k_pool, v_pool, lf_pool, sf_pool, l)
        o_mem_s = _mem_attn(zs3, mem_k_s, mem_v_s, layer=l, kcol=0, vcol=0, out_dtype=F32)
        o_gla_s, sg_s = _gla_step(zs3, w2p, b_gla_a[l], g_gla[l], state_gla, l)
        o_ret_s, sr_s = _ret_step(zs3, cos_s, sin_s, lg_tab, g_ret[l], state_ret, l)
        xs = _dense_tail(xs, [to_bf16_rows(a) for a in (o_fox_s, o_gla_s, o_ret_s, o_mem_s)], zs, wb, l)
        outs["fk_s"].append(zs3[:, :, OFF_FK:OFF_FK + FOX_W].reshape(bs, 1, FOX_HEADS, FOX_DH))
        outs["fv_s"].append(zs3[:, :, OFF_FV:OFF_FV + FOX_W].reshape(bs, 1, FOX_HEADS, FOX_DH))
        outs["fl_s"].append(lf_s.reshape(bs, 1, FOX_HEADS))
        outs["sg_s"].append(sg_s)
        outs["sr_s"].append(sr_s)

    y_prompt = _rmsnorm(xp, g_final).reshape(bp, t, d)
    y_sample = _rmsnorm(xs, g_final)[:bs].reshape(bs, 1, d)
    st = lambda k: jnp.stack(outs[k])
    fk_p = fk_stack.reshape(depth, bp, t, FOX_HEADS, FOX_DH)
    fv_p = fv_stack.reshape(depth, bp, t, FOX_HEADS, FOX_DH)
    return (y_prompt, y_sample, fk_p, fv_p, st("fl_p"), st("sg_p"), st("sr_p"), st("mk_p"), st("mv_p"),
            st("fk_s"), st("fv_s"), st("fl_s"), st("sg_s"), st("sr_s"))
```

```python
import functools

import numpy as np
import jax
import jax.numpy as jnp
from jax import lax
from jax.experimental import pallas as pl
from jax.experimental.pallas import tpu as pltpu

F32 = jnp.float32
BF16 = jnp.bfloat16

D_MODEL = 2048
FOX_HEADS = 8
FOX_DH = D_MODEL // 16
FOX_W = FOX_HEADS * FOX_DH
GLA_HEADS = 4
GLA_DK = D_MODEL // 16
GLA_DV = D_MODEL // 8
GLA_KW = GLA_HEADS * GLA_DK
GLA_VW = GLA_HEADS * GLA_DV
GLA_RANK = 16
GLA_TAU = 16.0
RET_HEADS = 4
RET_DH = D_MODEL // 8
RET_W = RET_HEADS * RET_DH
MEM_HEADS = 4
MEM_DH = D_MODEL // 8
MEM_W = MEM_HEADS * MEM_DH
N_BRANCH = 4
BRANCH_W = D_MODEL // 2
D_FF = 4 * D_MODEL
PAGE_SIZE = 128
ROPE_BASE = 10000.0
EPS = 1e-6
LOG2E = 1.4426950408889634

LANES = 128
SUBLANES = 8
BF16_ROWS = 16
VMEM_LIMIT_BYTES = 56 * 1024 * 1024

OFF_FQ = 0
OFF_FK = OFF_FQ + FOX_W
OFF_FV = OFF_FK + FOX_W
OFF_GQ = OFF_FV + FOX_W
OFF_GK = OFF_GQ + GLA_KW
OFF_GV = OFF_GK + GLA_KW
OFF_GR = OFF_GV + GLA_VW
OFF_RQ = OFF_GR + GLA_VW
OFF_RK = OFF_RQ + RET_W
OFF_RV = OFF_RK + RET_W
OFF_RG = OFF_RV + RET_W
OFF_MQ = OFF_RG + RET_W
OFF_GT = OFF_MQ + MEM_W
OFF_SM = OFF_GT + N_BRANCH * D_MODEL
SM_FF = 0
SM_GA = FOX_HEADS
N_PACK = OFF_SM + LANES

GLA_CHUNK = 64
GLA_SUB = 8
GLA_SUB_LOG2 = 3
GLA_CHUNKS_PER_STEP = 4
FOX_PAGES_PER_STEP = 8
FOX_DH_LOG2 = 7
assert GLA_SUB == 1 << GLA_SUB_LOG2 and FOX_DH == 1 << FOX_DH_LOG2 and LANES & (LANES - 1) == 0
RET_CHUNK = 256
REPACK_ROWS = 1024


def _cparams(sem):
    return pltpu.CompilerParams(dimension_semantics=sem, vmem_limit_bytes=VMEM_LIMIT_BYTES)


def _log_sigmoid(x):
    return jnp.minimum(x, 0.0) - jnp.log1p(jnp.exp(-jnp.abs(x)))


def _sigmoid(x):
    return 1.0 / (1.0 + jnp.exp(-x))


def _dot(a, b):
    return jnp.dot(a, b, preferred_element_type=F32)


def _dot_nt(a, b):
    return lax.dot_general(a, b, (((1,), (1,)), ((), ())), preferred_element_type=F32)


def _dot_tn(a, b):
    return lax.dot_general(a, b, (((0,), (0,)), ((), ())), preferred_element_type=F32)


def _dot01(m01, x):
    hi = x.astype(BF16)
    r1 = x - hi.astype(F32)
    mid = r1.astype(BF16)
    lo = (r1 - mid.astype(F32)).astype(BF16)
    return _dot(m01, hi) + _dot(m01, mid) + _dot(m01, lo)


def _row_to_cols(row):
    return jnp.broadcast_to(row, (LANES, LANES)).T


def _head_norm_gate(o, gain, gate):
    mu = jnp.mean(o, axis=1, keepdims=True)
    d = o - mu
    var = jnp.mean(d * d, axis=1, keepdims=True)
    y = d * lax.rsqrt(var + EPS) * gain
    return y * (gate * _sigmoid(gate))


def _mm_body(*refs, norm, act, has_res, stage, bf16_copy, w_rows):
    it = iter(refs)
    x_ref = next(it)
    g_ref = next(it) if norm else None
    w_ref = next(it)
    r_ref = next(it) if has_res else None
    o_ref = next(it)
    ob_ref = next(it) if bf16_copy else None
    h_ref = next(it) if stage else None
    if stage:
        @pl.when(pl.program_id(1) == 0)
        def _():
            x = x_ref[...].astype(F32)
            if norm:
                ms = jnp.mean(x * x, axis=-1, keepdims=True)
                x = x * lax.rsqrt(ms + EPS) * g_ref[...]
            h_ref[...] = x.astype(BF16)

        h = h_ref[...]
    else:
        h = x_ref[...]
    acc = _dot_nt(h, w_ref[...]) if w_rows else _dot(h, w_ref[...])
    if act == "relu2":
        acc = jnp.square(jnp.maximum(acc, 0.0))
    if has_res:
        acc = acc + r_ref[...]
    o_ref[...] = acc.astype(o_ref.dtype)
    if bf16_copy:
        ob_ref[...] = acc.astype(BF16)


def _pick_tile(n, candidates):
    for c in candidates:
        if n % c == 0:
            return c
    return n


def _matmul(x, w, *, layer=None, w_rows=False, g=None, res=None, act=None, out_dtype=F32, bf16_copy=False,
            name="proj"):
    m, k = x.shape
    n = w.shape[0] if w_rows else w.shape[-1]
    norm = g is not None
    stage = norm or x.dtype != BF16
    if m >= 1024:
        tm = 1024 if k <= 2048 else 512
    else:
        tm = m
    tn = _pick_tile(n, (1152, 1024, 512))
    if k > 2048:
        tn = min(tn, 512)
    assert m % tm == 0 and n % tn == 0 and tm % BF16_ROWS == 0
    in_specs = [pl.BlockSpec((tm, k), lambda i, j: (i, 0))]
    args = [x]
    if norm:
        in_specs.append(pl.BlockSpec((1, k), lambda i, j: (0, 0)))
        args.append(g.reshape(1, k).astype(F32))
    if w_rows:
        in_specs.append(pl.BlockSpec((tn, k), lambda i, j: (j, 0)))
    elif layer is None:
        in_specs.append(pl.BlockSpec((k, tn), lambda i, j: (0, j)))
    else:
        in_specs.append(pl.BlockSpec((None, k, tn), lambda i, j: (layer, 0, j)))
    args.append(w)
    if res is not None:
        in_specs.append(pl.BlockSpec((tm, tn), lambda i, j: (i, j)))
        args.append(res)
    scratch = [pltpu.VMEM((tm, k), BF16)] if stage else []
    o_spec = pl.BlockSpec((tm, tn), lambda i, j: (i, j))
    o_shape = jax.ShapeDtypeStruct((m, n), out_dtype)
    return pl.pallas_call(
        functools.partial(_mm_body, norm=norm, act=act, has_res=res is not None, stage=stage, bf16_copy=bf16_copy,
                          w_rows=w_rows),
        grid=(m // tm, n // tn),
        in_specs=in_specs,
        out_specs=[o_spec, o_spec] if bf16_copy else o_spec,
        out_shape=[o_shape, jax.ShapeDtypeStruct((m, n), BF16)] if bf16_copy else o_shape,
        scratch_shapes=scratch,
        compiler_params=_cparams(("parallel", "arbitrary")),
        name=name,
    )(*args)


def _rmsnorm_body(x_ref, g_ref, o_ref):
    x = x_ref[...]
    ms = jnp.mean(x * x, axis=-1, keepdims=True)
    o_ref[...] = x * lax.rsqrt(ms + EPS) * g_ref[...]


def _rmsnorm(x, g):
    m, k = x.shape
    tm = _pick_tile(m, (512,))
    return pl.pallas_call(
        _rmsnorm_body,
        grid=(m // tm,),
        in_specs=[pl.BlockSpec((tm, k), lambda i: (i, 0)), pl.BlockSpec((1, k), lambda i: (0, 0))],
        out_specs=pl.BlockSpec((tm, k), lambda i: (i, 0)),
        out_shape=jax.ShapeDtypeStruct((m, k), F32),
        compiler_params=_cparams(("parallel",)),
        name="final_norm",
    )(x, g.reshape(1, k))


def _fox_gate_body(x_ref, b_ref, lf_ref, c_ref):
    lf = _log_sigmoid(x_ref[...] + b_ref[...])
    lf_ref[...] = lf
    t = lf.shape[1]
    lane = lax.broadcasted_iota(jnp.int32, lf.shape, 1)
    c = lf
    shift = 1
    while shift < t:
        c = c + jnp.where(lane >= shift, pltpu.roll(c, shift, axis=1), 0.0)
        shift *= 2
    c_ref[...] = c


def _fox_gate(ff_t, bias):
    b, h, t = ff_t.shape
    spec = pl.BlockSpec((None, h, t), lambda i: (i, 0, 0))
    return pl.pallas_call(
        _fox_gate_body,
        grid=(b,),
        in_specs=[spec, pl.BlockSpec((h, 1), lambda i: (0, 0))],
        out_specs=[spec, spec],
        out_shape=[jax.ShapeDtypeStruct((b, h, t), F32)] * 2,
        compiler_params=_cparams(("parallel",)),
        name="fox_gate",
    )(ff_t, bias.reshape(h, 1))


def _fox_attn_body(q_ref, k_ref, v_ref, ck_ref, o_ref, q_sc, m_sc, l_sc, acc_sc, *, tile):
    qi = pl.program_id(1)
    ki = pl.program_id(2)
    nk = pl.num_programs(2)

    @pl.when(ki == 0)
    def _():
        q_sc[...] = (q_ref[...] * (FOX_DH ** -0.5 * LOG2E)).astype(BF16)
        m_sc[...] = jnp.full(m_sc.shape, -jnp.inf, F32)
        l_sc[...] = jnp.zeros(l_sc.shape, F32)
        acc_sc[...] = jnp.zeros(acc_sc.shape, F32)

    def step(on_diagonal):
        ck = ck_ref[...] * LOG2E
        ones = jnp.ones((tile, FOX_DH), BF16)
        if on_diagonal:
            row = lax.broadcasted_iota(jnp.int32, (tile, tile), 0)
            col = lax.broadcasted_iota(jnp.int32, (tile, tile), 1)
            visible = col <= row
        for h in range(FOX_HEADS):
            sl = slice(h * FOX_DH, (h + 1) * FOX_DH)
            s = _dot_nt(q_sc[:, sl], k_ref[:, sl]) - ck[h:h + 1, :]
            if on_diagonal:
                s = jnp.where(visible, s, -jnp.inf)
            m_prev = m_sc[h]
            m_new = jnp.maximum(m_prev, jnp.max(s, axis=1, keepdims=True))
            alpha = jnp.exp2(m_prev - m_new)
            p = jnp.exp2(s - jnp.concatenate([m_new] * (tile // FOX_DH), axis=1)).astype(BF16)
            pv = _dot(p, jnp.concatenate([v_ref[:, sl], ones], axis=1))
            l_sc[h] = alpha * l_sc[h] + pv[:, FOX_DH:]
            acc_sc[:, sl] = alpha * acc_sc[:, sl] + pv[:, :FOX_DH]
            m_sc[h] = m_new

    pl.when(ki < qi)(functools.partial(step, False))
    pl.when(ki == qi)(functools.partial(step, True))

    @pl.when(ki == nk - 1)
    def _():
        for h in range(FOX_HEADS):
            sl = slice(h * FOX_DH, (h + 1) * FOX_DH)
            o_ref[:, sl] = (acc_sc[:, sl] / l_sc[h]).astype(o_ref.dtype)


def _fox_attn(z3, zb3, c_row):
    b, t, _ = z3.shape
    tile = _pick_tile(t, (512, 256, 128))
    n = t // tile
    qcol, kcol, vcol = OFF_FQ // FOX_W, OFF_FK // FOX_W, OFF_FV // FOX_W
    return pl.pallas_call(
        functools.partial(_fox_attn_body, tile=tile),
        grid=(b, n, n),
        in_specs=[
            pl.BlockSpec((None, tile, FOX_W), lambda bi, qi, ki: (bi, qi, qcol)),
            pl.BlockSpec((None, tile, FOX_W), lambda bi, qi, ki: (bi, jnp.minimum(ki, qi), kcol)),
            pl.BlockSpec((None, tile, FOX_W), lambda bi, qi, ki: (bi, jnp.minimum(ki, qi), vcol)),
            pl.BlockSpec((None, FOX_HEADS, tile), lambda bi, qi, ki: (bi, 0, jnp.minimum(ki, qi))),
        ],
        out_specs=pl.BlockSpec((None, tile, FOX_W), lambda bi, qi, ki: (bi, qi, 0)),
        out_shape=jax.ShapeDtypeStruct((b, t, FOX_W), BF16),
        scratch_shapes=[
            pltpu.VMEM((tile, FOX_W), BF16),
            pltpu.VMEM((FOX_HEADS, tile, FOX_DH), F32),
            pltpu.VMEM((FOX_HEADS, tile, FOX_DH), F32),
            pltpu.VMEM((tile, FOX_W), F32),
        ],
        compiler_params=_cparams(("parallel", "parallel", "arbitrary")),
        name="fox_attn",
    )(z3, zb3, zb3, c_row)


def _mem_attn_body(q_ref, k_ref, v_ref, o_ref, *, rows):
    scale = MEM_DH ** -0.5
    for h in range(MEM_HEADS):
        sl = slice(h * MEM_DH, (h + 1) * MEM_DH)
        q = q_ref[:, sl]
        if rows < SUBLANES:
            q = jnp.broadcast_to(q, (SUBLANES, MEM_DH))
        s = _dot_nt(q.astype(BF16), k_ref[:, sl].astype(BF16)) * scale
        m = jnp.max(s, axis=1, keepdims=True)
        p = jnp.exp(s - m)
        p = p / jnp.sum(p, axis=1, keepdims=True)
        o = _dot(p.astype(BF16), v_ref[:, sl].astype(BF16))
        o_ref[:, sl] = o[:rows].astype(o_ref.dtype)


def _mem_attn(z3, k4, v4, *, layer, kcol, vcol, out_dtype):
    b, t, _ = z3.shape
    n_mem = k4.shape[2]
    tq = _pick_tile(t, (512, 256, 128))
    qcol = OFF_MQ // MEM_W
    return pl.pallas_call(
        functools.partial(_mem_attn_body, rows=tq),
        grid=(b, t // tq),
        in_specs=[
            pl.BlockSpec((None, tq, MEM_W), lambda bi, qi: (bi, qi, qcol)),
            pl.BlockSpec((None, None, n_mem, MEM_W), lambda bi, qi: (layer, bi, 0, kcol)),
            pl.BlockSpec((None, None, n_mem, MEM_W), lambda bi, qi: (layer, bi, 0, vcol)),
        ],
        out_specs=pl.BlockSpec((None, tq, MEM_W), lambda bi, qi: (bi, qi, 0)),
        out_shape=jax.ShapeDtypeStruct((b, t, MEM_W), out_dtype),
        compiler_params=_cparams(("parallel", "parallel")),
        name="mem_attn",
    )(z3, k4, v4)


def _rotary(x, cos, sin):
    half = x.shape[1] // 2
    x1, x2 = x[:, :half], x[:, half:]
    return jnp.concatenate([x1 * cos - x2 * sin, x1 * sin + x2 * cos], axis=1)


def _ret_body(q_ref, k_ref, v_ref, g_ref, cos_ref, sin_ref, lg_ref, gn_ref, o_ref, s_ref, st_sc, *, chunk):
    c = pl.program_id(1)
    nc = pl.num_programs(1)

    @pl.when(c == 0)
    def _():
        st_sc[...] = jnp.zeros(st_sc.shape, F32)

    cos, sin = cos_ref[...], sin_ref[...]
    row = lax.broadcasted_iota(jnp.int32, (chunk, chunk), 0)
    col = lax.broadcasted_iota(jnp.int32, (chunk, chunk), 1)
    lag = (row - col).astype(F32)
    t = lax.broadcasted_iota(jnp.int32, (chunk, 1), 0).astype(F32)
    for h in range(RET_HEADS):
        sl = slice(h * RET_DH, (h + 1) * RET_DH)
        lg = lg_ref[h][:, :1]
        q = _rotary(q_ref[:, sl], cos, sin)
        k = _rotary(k_ref[:, sl], cos, sin) * (RET_DH ** -0.5)
        v = v_ref[:, sl].astype(BF16)
        decay = jnp.where(col <= row, jnp.exp(lag * lg), 0.0)
        a = _dot_nt(q.astype(BF16), k.astype(BF16)) * decay
        s0 = st_sc[h]
        o = _dot(a.astype(BF16), v) + jnp.exp((t + 1.0) * lg) * _dot(q.astype(BF16), s0.astype(BF16))
        k_end = k * jnp.exp((chunk - 1.0 - t) * lg)
        s_new = jnp.exp(chunk * lg) * s0 + _dot_tn(k_end.astype(BF16), v)
        st_sc[h] = s_new
        o_ref[:, sl] = _head_norm_gate(o, gn_ref[:, sl], g_ref[:, sl]).astype(o_ref.dtype)

        @pl.when(c == nc - 1)
        def _():
            s_ref[h] = s_new


def _ret_prompt(z3, cos, sin, lg_tab, g_ret):
    b, t, _ = z3.shape
    chunk = _pick_tile(t, (RET_CHUNK, 128, 64))
    zspec = lambda off: pl.BlockSpec((None, chunk, RET_W), lambda bi, c: (bi, c, off // RET_W))
    return pl.pallas_call(
        functools.partial(_ret_body, chunk=chunk),
        grid=(b, t // chunk),
        in_specs=[
            zspec(OFF_RQ), zspec(OFF_RK), zspec(OFF_RV), zspec(OFF_RG),
            pl.BlockSpec((chunk, RET_DH // 2), lambda bi, c: (c, 0)),
            pl.BlockSpec((chunk, RET_DH // 2), lambda bi, c: (c, 0)),
            pl.BlockSpec((RET_HEADS, 1, LANES), lambda bi, c: (0, 0, 0)),
            pl.BlockSpec((1, RET_W), lambda bi, c: (0, 0)),
        ],
        out_specs=[
            pl.BlockSpec((None, chunk, RET_W), lambda bi, c: (bi, c, 0)),
            pl.BlockSpec((None, RET_HEADS, RET_DH, RET_DH), lambda bi, c: (bi, 0, 0, 0)),
        ],
        out_shape=[
            jax.ShapeDtypeStruct((b, t, RET_W), BF16),
            jax.ShapeDtypeStruct((b, RET_HEADS, RET_DH, RET_DH), F32),
        ],
        scratch_shapes=[pltpu.VMEM((RET_HEADS, RET_DH, RET_DH), F32)],
        compiler_params=_cparams(("parallel", "arbitrary")),
        name="ret_prompt",
    )(z3, z3, z3, z3, cos, sin, lg_tab, g_ret.reshape(1, RET_W))


def _gla_loga(zs, w2_ref, ba_ref):
    pre = _dot(zs.astype(BF16), w2_ref[...]) + ba_ref[...]
    return _log_sigmoid(pre) * (1.0 / GLA_TAU)


def _gla_chunk(q, k, v_ref, gr_ref, gn_ref, o_ref, loga, st, tri, hsum, rows):
    chunk = q.shape[0]
    n_sub = chunk // GLA_SUB
    b = _dot01(tri, loga)
    row = lax.broadcasted_iota(jnp.int32, (chunk, GLA_KW), 0)
    b_last = b[chunk - 1:chunk, :]
    beta = jnp.zeros_like(b)
    for i in range(1, n_sub):
        beta = jnp.where(row >= i * GLA_SUB, b[i * GLA_SUB - 1:i * GLA_SUB, :], beta)
    q_rel = q * jnp.exp(b - beta)
    q_abs = q * jnp.exp(b)
    k_end = k * jnp.exp(b_last - b)
    k_rel = []
    for i in range(1, n_sub):
        bi = b[i * GLA_SUB - 1:i * GLA_SUB, :]
        k_rel.append(jnp.where(row < i * GLA_SUB, k * jnp.exp(jnp.minimum(bi - b, 0.0)), 0.0).astype(BF16))

    prods = []
    for delta in range(GLA_SUB):
        kd = k if delta == 0 else pltpu.roll(k, delta, axis=0)
        bd = b if delta == 0 else pltpu.roll(b, delta, axis=0)
        prods.append((q * kd * jnp.exp(jnp.minimum(b - bd, 0.0))).astype(BF16))
    sums = _dot(jnp.concatenate(prods, axis=0), hsum)
    rowb = lax.broadcasted_iota(jnp.int32, (chunk, GLA_HEADS * LANES), 0)
    colb = lax.broadcasted_iota(jnp.int32, (chunk, GLA_HEADS * LANES), 1) & (LANES - 1)
    rsub = rowb & (GLA_SUB - 1)
    band = jnp.zeros((chunk, GLA_HEADS * LANES), F32)
    for delta in range(GLA_SUB):
        keep = jnp.logical_and(colb == rowb - delta, rsub >= delta)
        band = band + jnp.where(keep, sums[delta * chunk:(delta + 1) * chunk], 0.0)

    sub_of_row = row[:, :GLA_DK] >> GLA_SUB_LOG2
    new_st = []
    for h in range(GLA_HEADS):
        ks = slice(h * GLA_DK, (h + 1) * GLA_DK)
        vs = slice(h * GLA_DV, (h + 1) * GLA_DV)
        v = v_ref[rows, vs].astype(BF16)
        lhs = jnp.concatenate(
            [jnp.where(sub_of_row == i, q_rel[:, ks], 0.0) for i in range(1, n_sub)], axis=1).astype(BF16)
        rhs = jnp.concatenate([kr[:, ks] for kr in k_rel], axis=1)
        a = _dot_nt(lhs, rhs) + band[:, h * LANES:h * LANES + chunk]
        o = _dot(a.astype(BF16), v) + _dot_nt(q_abs[:, ks].astype(BF16), st[h].astype(BF16))
        new_st.append(st[h] * jnp.exp(b_last[:, ks]) + _dot_tn(v, k_end[:, ks].astype(BF16)))
        o_ref[rows, vs] = _head_norm_gate(o, gn_ref[:, vs], gr_ref[rows, vs]).astype(o_ref.dtype)
    return new_st


def _gla_body(q_ref, k_ref, v_ref, gr_ref, zs_ref, w2_ref, ba_ref, gn_ref, tri_ref, hsum_ref,
              o_ref, s_ref, st_sc, *, chunk, n_chunks):
    c = pl.program_id(1)
    nc = pl.num_programs(1)

    @pl.when(c == 0)
    def _():
        st_sc[...] = jnp.zeros(st_sc.shape, F32)

    loga = _gla_loga(zs_ref[...], w2_ref, ba_ref)
    tri, hsum = tri_ref[...], hsum_ref[...]
    st = [st_sc[h] for h in range(GLA_HEADS)]
    for ci in range(n_chunks):
        rows = slice(ci * chunk, (ci + 1) * chunk)
        q = q_ref[rows, :] * (GLA_DK ** -0.5)
        st = _gla_chunk(q, k_ref[rows, :], v_ref, gr_ref, gn_ref, o_ref, loga[rows], st, tri, hsum, rows)
    for h in range(GLA_HEADS):
        st_sc[h] = st[h]

    @pl.when(c == nc - 1)
    def _():
        for h in range(GLA_HEADS):
            s_ref[h] = st[h].T


def _gla_consts(chunk):
    r = np.arange(chunk)
    tri = (r[None, :] <= r[:, None]).astype(np.float32)
    lane = np.arange(GLA_KW)
    out = np.arange(GLA_HEADS * LANES)
    hsum = (lane[:, None] // GLA_DK == out[None, :] // LANES).astype(np.float32)
    return jnp.asarray(tri, BF16), jnp.asarray(hsum, BF16)


def _gla_prompt(z3, w2p, b_a, g_gla):
    b, t, _ = z3.shape
    chunk = GLA_CHUNK
    n_chunks = _pick_tile(t // chunk, (GLA_CHUNKS_PER_STEP, 2, 1))
    rows = chunk * n_chunks
    assert t % rows == 0
    tri, hsum = _gla_consts(chunk)
    full = lambda shape: pl.BlockSpec(shape, lambda bi, c: (0,) * len(shape))
    return pl.pallas_call(
        functools.partial(_gla_body, chunk=chunk, n_chunks=n_chunks),
        grid=(b, t // rows),
        in_specs=[
            pl.BlockSpec((None, rows, GLA_KW), lambda bi, c: (bi, c, OFF_GQ // GLA_KW)),
            pl.BlockSpec((None, rows, GLA_KW), lambda bi, c: (bi, c, OFF_GK // GLA_KW)),
            pl.BlockSpec((None, rows, GLA_VW), lambda bi, c: (bi, c, OFF_GV // GLA_VW)),
            pl.BlockSpec((None, rows, GLA_VW), lambda bi, c: (bi, c, OFF_GR // GLA_VW)),
            pl.BlockSpec((None, rows, LANES), lambda bi, c: (bi, c, OFF_SM // LANES)),
            full((LANES, GLA_KW)), full((1, GLA_KW)), full((1, GLA_VW)),
            full((chunk, chunk)), full((GLA_KW, GLA_HEADS * LANES)),
        ],
        out_specs=[
            pl.BlockSpec((None, rows, GLA_VW), lambda bi, c: (bi, c, 0)),
            pl.BlockSpec((None, GLA_HEADS, GLA_DK, GLA_DV), lambda bi, c: (bi, 0, 0, 0)),
        ],
        out_shape=[
            jax.ShapeDtypeStruct((b, t, GLA_VW), BF16),
            jax.ShapeDtypeStruct((b, GLA_HEADS, GLA_DK, GLA_DV), F32),
        ],
        scratch_shapes=[pltpu.VMEM((GLA_HEADS, GLA_DV, GLA_DK), F32)],
        compiler_params=_cparams(("parallel", "arbitrary")),
        name="gla_prompt",
    )(z3, z3, z3, z3, z3, w2p, b_a.reshape(1, GLA_KW), g_gla.reshape(1, GLA_VW), tri, hsum)


def _gla_step_body(q_ref, k_ref, v_ref, gr_ref, zs_ref, w2_ref, ba_ref, gn_ref, s0_ref, o_ref, s_ref):
    zs = jnp.broadcast_to(zs_ref[...], (BF16_ROWS, LANES))
    loga = _gla_loga(zs, w2_ref, ba_ref)[:1]
    q = q_ref[...] * (GLA_DK ** -0.5)
    k = k_ref[...]
    for h in range(GLA_HEADS):
        ks = slice(h * GLA_DK, (h + 1) * GLA_DK)
        vs = slice(h * GLA_DV, (h + 1) * GLA_DV)
        widen = lambda m: jnp.concatenate([m] * (GLA_DV // LANES), axis=1)
        a_c = widen(_row_to_cols(jnp.exp(loga[:, ks])))
        k_c = widen(_row_to_cols(k[:, ks]))
        q_c = widen(_row_to_cols(q[:, ks]))
        s_new = a_c * s0_ref[h] + k_c * v_ref[:, vs]
        s_ref[h] = s_new
        o = jnp.sum(q_c * s_new, axis=0, keepdims=True)
        o_ref[:, vs] = _head_norm_gate(o, gn_ref[:, vs], gr_ref[:, vs])


def _gla_step(zs3, w2p, b_a, g_gla, state, layer):
    b = zs3.shape[0]
    full = lambda shape: pl.BlockSpec(shape, lambda bi: (0,) * len(shape))
    return pl.pallas_call(
        _gla_step_body,
        grid=(b,),
        in_specs=[
            pl.BlockSpec((None, 1, GLA_KW), lambda bi: (bi, 0, OFF_GQ // GLA_KW)),
            pl.BlockSpec((None, 1, GLA_KW), lambda bi: (bi, 0, OFF_GK // GLA_KW)),
            pl.BlockSpec((None, 1, GLA_VW), lambda bi: (bi, 0, OFF_GV // GLA_VW)),
            pl.BlockSpec((None, 1, GLA_VW), lambda bi: (bi, 0, OFF_GR // GLA_VW)),
            pl.BlockSpec((None, 1, LANES), lambda bi: (bi, 0, OFF_SM // LANES)),
            full((LANES, GLA_KW)), full((1, GLA_KW)), full((1, GLA_VW)),
            pl.BlockSpec((None, None, GLA_HEADS, GLA_DK, GLA_DV), lambda bi: (layer, bi, 0, 0, 0)),
        ],
        out_specs=[
            pl.BlockSpec((None, 1, GLA_VW), lambda bi: (bi, 0, 0)),
            pl.BlockSpec((None, GLA_HEADS, GLA_DK, GLA_DV), lambda bi: (bi, 0, 0, 0)),
        ],
        out_shape=[
            jax.ShapeDtypeStruct((b, 1, GLA_VW), F32),
            jax.ShapeDtypeStruct((b, GLA_HEADS, GLA_DK, GLA_DV), F32),
        ],
        compiler_params=_cparams(("parallel",)),
        name="gla_step",
    )(zs3, zs3, zs3, zs3, zs3, w2p, b_a.reshape(1, GLA_KW), g_gla.reshape(1, GLA_VW), state)


def _ret_step_body(q_ref, k_ref, v_ref, g_ref, cos_ref, sin_ref, lg_ref, gn_ref, s0_ref, o_ref, s_ref):
    cos, sin = cos_ref[...], sin_ref[...]
    half = RET_DH // 2
    for h in range(RET_HEADS):
        sl = slice(h * RET_DH, (h + 1) * RET_DH)
        q = _rotary(q_ref[:, sl], cos, sin)
        k = _rotary(k_ref[:, sl], cos, sin) * (RET_DH ** -0.5)
        cols = lambda r: jnp.concatenate(
            [jnp.concatenate([_row_to_cols(r[:, i * half:(i + 1) * half])] * 2, axis=1) for i in range(2)], axis=0)
        gamma = jnp.exp(lg_ref[h][:, :1])
        s_new = gamma * s0_ref[h] + cols(k) * v_ref[:, sl]
        s_ref[h] = s_new
        o = jnp.sum(cols(q) * s_new, axis=0, keepdims=True)
        o_ref[:, sl] = _head_norm_gate(o, gn_ref[:, sl], g_ref[:, sl])


def _ret_step(zs3, cos, sin, lg_tab, g_ret, state, layer):
    b = zs3.shape[0]
    full = lambda shape: pl.BlockSpec(shape, lambda bi: (0,) * len(shape))
    zspec = lambda off: pl.BlockSpec((None, 1, RET_W), lambda bi: (bi, 0, off // RET_W))
    return pl.pallas_call(
        _ret_step_body,
        grid=(b,),
        in_specs=[
            zspec(OFF_RQ), zspec(OFF_RK), zspec(OFF_RV), zspec(OFF_RG),
            full((1, RET_DH // 2)), full((1, RET_DH // 2)), full((RET_HEADS, 1, LANES)), full((1, RET_W)),
            pl.BlockSpec((None, None, RET_HEADS, RET_DH, RET_DH), lambda bi: (layer, bi, 0, 0, 0)),
        ],
        out_specs=[
            pl.BlockSpec((None, 1, RET_W), lambda bi: (bi, 0, 0)),
            pl.BlockSpec((None, RET_HEADS, RET_DH, RET_DH), lambda bi: (bi, 0, 0, 0)),
        ],
        out_shape=[
            jax.ShapeDtypeStruct((b, 1, RET_W), F32),
            jax.ShapeDtypeStruct((b, RET_HEADS, RET_DH, RET_DH), F32),
        ],
        compiler_params=_cparams(("parallel",)),
        name="ret_step",
    )(zs3, zs3, zs3, zs3, cos, sin, lg_tab, g_ret.reshape(1, RET_W), state)


def _page_suffix_body(x_ref, o_ref):
    x = x_ref[...]
    width = x.shape[1]
    lane = lax.broadcasted_iota(jnp.int32, x.shape, 1)
    incl = x
    shift = FOX_HEADS
    while shift < width:
        moved = pltpu.roll(incl, width - shift, axis=1)
        incl = incl + jnp.where(lane < width - shift, moved, 0.0)
        shift *= 2
    o_ref[...] = incl


def _page_suffix(lf_pages):
    n, width = lf_pages.shape
    rows = _pick_tile(n, (64, 32, 16, 8))
    spec = pl.BlockSpec((rows, width), lambda i: (i, 0))
    return pl.pallas_call(
        _page_suffix_body,
        grid=(n // rows,),
        in_specs=[spec],
        out_specs=spec,
        out_shape=jax.ShapeDtypeStruct((n, width), F32),
        compiler_params=_cparams(("parallel",)),
        name="page_suffix",
    )(lf_pages)


def _fox_decode_body(*refs, pages, layer, n_pool, n_pages):
    q_ref, kn_ref, vn_ref, ff_ref, bf_ref = refs[1:6]
    kp_refs = refs[6:6 + pages]
    vp_refs = refs[6 + pages:6 + 2 * pages]
    lp_refs = refs[6 + 2 * pages:6 + 3 * pages]
    sf_refs = refs[6 + 3 * pages:6 + 4 * pages]
    o_ref, lf_ref, m_sc, l_sc, tail_sc, acc_sc = refs[6 + 4 * pages:]
    g = pl.program_id(1)
    n_groups = pl.num_programs(1)
    scale = FOX_DH ** -0.5
    width = PAGE_SIZE * FOX_HEADS
    sub = lax.broadcasted_iota(jnp.int32, (FOX_HEADS, width), 0)
    lane = lax.broadcasted_iota(jnp.int32, (FOX_HEADS, width), 1)
    own = (lane & (FOX_HEADS - 1)) == sub
    q = q_ref[...]
    qb = q.astype(BF16)

    @pl.when(g == 0)
    def _():
        lf_new = _log_sigmoid(ff_ref[...] + bf_ref[...])
        lf_ref[...] = lf_new
        tail_sc[...] = jnp.broadcast_to(lf_new, tail_sc.shape)
        s_new = jnp.sum(q * kn_ref[...], axis=1, keepdims=True) * scale
        m_sc[...] = jnp.broadcast_to(s_new, m_sc.shape)
        l_sc[...] = jnp.ones(l_sc.shape, F32)
        acc_sc[...] = vn_ref[...]

    tail = tail_sc[:, :1]
    scores = []
    pt_ref = refs[0]
    bi = pl.program_id(0)
    for i in range(pages):
        flat = layer * n_pool + pt_ref[bi, n_pages - 1 - (g * pages + i)]
        r = flat % SUBLANES
        incl = jnp.broadcast_to(sf_refs[i][pl.ds(r, 1), :], (FOX_HEADS, width))
        bias = tail + (incl - lp_refs[i][pl.ds(r, 1), :])
        s = _dot_nt(qb, kp_refs[i][...].astype(BF16)) * scale + bias
        scores.append(jnp.where(own, s, -jnp.inf))
        tail = tail + jnp.sum(jnp.where(lane == sub, incl, 0.0), axis=1, keepdims=True)
    m_prev = m_sc[:, :1]
    m_new = m_prev
    for s in scores:
        m_new = jnp.maximum(m_new, jnp.max(s, axis=1, keepdims=True))
    alpha = jnp.exp(m_prev - m_new)
    l_new = alpha * l_sc[:, :1]
    acc = alpha * acc_sc[...]
    for i in range(pages):
        pr = jnp.exp(scores[i] - m_new)
        l_new = l_new + jnp.sum(pr, axis=1, keepdims=True)
        acc = acc + _dot(pr.astype(BF16), vp_refs[i][...].astype(BF16))
    acc_sc[...] = acc
    l_sc[...] = jnp.broadcast_to(l_new, l_sc.shape)
    m_sc[...] = jnp.broadcast_to(m_new, m_sc.shape)
    tail_sc[...] = jnp.broadcast_to(tail, tail_sc.shape)

    @pl.when(g == n_groups - 1)
    def _():
        o_ref[...] = acc / l_new


def _fox_decode(page_table, q8, kn8, vn8, ff_col, bf_col, k_pool, v_pool, lf_pool, sf_pool, layer):
    b, n_pages = page_table.shape
    n_pool = k_pool.shape[1]
    width = PAGE_SIZE * FOX_HEADS
    pages = _pick_tile(n_pages, (FOX_PAGES_PER_STEP, 2, 1))

    def page(i):
        return lambda bi, g, pt: (layer, pt[bi, n_pages - 1 - (g * pages + i)], 0, 0)

    tok = pl.BlockSpec((None, FOX_HEADS, FOX_DH), lambda bi, g, pt: (bi, 0, 0))
    col = pl.BlockSpec((None, FOX_HEADS, 1), lambda bi, g, pt: (bi, 0, 0))
    kv_specs = [pl.BlockSpec((None, None, width, FOX_DH), page(i)) for i in range(pages)]
    def row_block(i):
        return lambda bi, g, pt: ((layer * n_pool + pt[bi, n_pages - 1 - (g * pages + i)]) // SUBLANES, 0)

    row_specs = [pl.BlockSpec((SUBLANES, width), row_block(i)) for i in range(pages)]
    grid_spec = pltpu.PrefetchScalarGridSpec(
        num_scalar_prefetch=1,
        grid=(b, n_pages // pages),
        in_specs=[tok, tok, tok, col, pl.BlockSpec((FOX_HEADS, 1), lambda bi, g, pt: (0, 0))]
        + kv_specs + kv_specs + row_specs + row_specs,
        out_specs=[tok, col],
        scratch_shapes=[
            pltpu.VMEM((FOX_HEADS, LANES), F32),
            pltpu.VMEM((FOX_HEADS, LANES), F32),
            pltpu.VMEM((FOX_HEADS, LANES), F32),
            pltpu.VMEM((FOX_HEADS, FOX_DH), F32),
        ],
    )
    return pl.pallas_call(
        functools.partial(_fox_decode_body, pages=pages, layer=layer, n_pool=n_pool, n_pages=n_pages),
        grid_spec=grid_spec,
        out_shape=[
            jax.ShapeDtypeStruct((b, FOX_HEADS, FOX_DH), F32),
            jax.ShapeDtypeStruct((b, FOX_HEADS, 1), F32),
        ],
        compiler_params=_cparams(("parallel", "arbitrary")),
        name="fox_decode",
    )(page_table, q8, kn8, vn8, ff_col, bf_col, *([k_pool] * pages), *([v_pool] * pages),
      *([lf_pool] * pages), *([sf_pool] * pages))


def _merge_body(o0, o1, o2, o3, g0, g1, g2, g3, w_ref, out_ref):
    acc = None
    for i, (o_ref, g_ref) in enumerate(((o0, g0), (o1, g1), (o2, g2), (o3, g3))):
        y = (0.5 * jnp.tanh(0.5 * g_ref[...].astype(F32)) + 0.5) * _dot(o_ref[...], w_ref[i])
        acc = y if acc is None else acc + y
    out_ref[...] = acc.astype(out_ref.dtype)


def _merge(branches, z, w_branch, layer):
    m = z.shape[0]
    tm = _pick_tile(m, (1024, 512))
    tn = 512
    assert D_MODEL % tn == 0 and OFF_GT % tn == 0
    o_spec = pl.BlockSpec((tm, BRANCH_W), lambda i, j: (i, 0))
    g_spec = lambda br: pl.BlockSpec((tm, tn), lambda i, j: (i, (OFF_GT + br * D_MODEL) // tn + j))
    return pl.pallas_call(
        _merge_body,
        grid=(m // tm, D_MODEL // tn),
        in_specs=[o_spec] * N_BRANCH + [g_spec(br) for br in range(N_BRANCH)]
        + [pl.BlockSpec((None, N_BRANCH, BRANCH_W, tn), lambda i, j: (layer, 0, 0, j))],
        out_specs=pl.BlockSpec((tm, tn), lambda i, j: (i, j)),
        out_shape=jax.ShapeDtypeStruct((m, D_MODEL), BF16),
        compiler_params=_cparams(("parallel", "arbitrary")),
        name="merge",
    )(*branches, z, z, z, z, w_branch)


def _heads_out_body(k_ref, v_ref, k_any, v_any, ko_ref, vo_ref):
    del k_any, v_any
    for h in range(FOX_HEADS):
        sl = slice(h * FOX_DH, (h + 1) * FOX_DH)
        ko_ref[:, h, :] = k_ref[:, sl]
        vo_ref[:, h, :] = v_ref[:, sl]


def _heads_out(z, k_stack, v_stack, layer):
    m = z.shape[0]
    tm = _pick_tile(m, (512,))
    o_spec = pl.BlockSpec((None, tm, FOX_HEADS, FOX_DH), lambda i: (layer, i, 0, 0))
    return pl.pallas_call(
        _heads_out_body,
        grid=(m // tm,),
        in_specs=[
            pl.BlockSpec((tm, FOX_W), lambda i: (i, OFF_FK // FOX_W)),
            pl.BlockSpec((tm, FOX_W), lambda i: (i, OFF_FV // FOX_W)),
            pl.BlockSpec(memory_space=pl.ANY),
            pl.BlockSpec(memory_space=pl.ANY),
        ],
        out_specs=[o_spec, o_spec],
        out_shape=[jax.ShapeDtypeStruct(k_stack.shape, F32), jax.ShapeDtypeStruct(v_stack.shape, F32)],
        input_output_aliases={2: 0, 3: 1},
        compiler_params=_cparams(("parallel",)),
        name="heads_out",
    )(z, z, k_stack, v_stack)


def _repack_body(src_ref, ff_ref, ga_ref, o_ref, *, n_big):
    j = pl.program_id(0)

    @pl.when(j < n_big)
    def _():
        o_ref[...] = src_ref[0].astype(BF16)

    @pl.when(j == n_big)
    def _():
        k = o_ref.shape[1]
        pad = jnp.zeros((o_ref.shape[0] - FOX_HEADS - GLA_RANK, k), F32)
        o_ref[...] = jnp.concatenate([ff_ref[0], ga_ref[0], pad], axis=0).astype(BF16)


def _pack_w_in(w_in_t, layer):
    _, d_in, k = w_in_t.shape
    c_ff = 3 * FOX_W
    c_ga = c_ff + FOX_HEADS + 2 * GLA_KW + 2 * GLA_VW
    n_big = OFF_SM // REPACK_ROWS
    assert OFF_GQ % REPACK_ROWS == 0 and OFF_RQ % REPACK_ROWS == 0 and OFF_SM % REPACK_ROWS == 0
    assert FOX_HEADS % SUBLANES == 0 and GLA_RANK % SUBLANES == 0 and SM_GA == SM_FF + FOX_HEADS

    def src_index(j):
        start = jnp.minimum(j, n_big - 1) * REPACK_ROWS
        shift = jnp.where(start < OFF_GQ, 0, jnp.where(start < OFF_RQ, FOX_HEADS, FOX_HEADS + GLA_RANK))
        return (layer, pl.multiple_of(start + shift, SUBLANES), 0)

    rows = lambda n, start: pl.BlockSpec((pl.Element(1), pl.Element(n), pl.Element(k)), lambda j: (layer, start, 0))
    return pl.pallas_call(
        functools.partial(_repack_body, n_big=n_big),
        grid=(n_big + 1,),
        in_specs=[
            pl.BlockSpec((pl.Element(1), pl.Element(REPACK_ROWS), pl.Element(k)), src_index),
            rows(FOX_HEADS, c_ff), rows(GLA_RANK, c_ga),
        ],
        out_specs=pl.BlockSpec((REPACK_ROWS, k), lambda j: (j, 0)),
        out_shape=jax.ShapeDtypeStruct((N_PACK, k), BF16),
        compiler_params=_cparams(("arbitrary",)),
        name="repack_w_in",
    )(w_in_t, w_in_t, w_in_t)


def _pack_w_gla_a2(w2):
    out = jnp.zeros((LANES, GLA_KW), F32).at[SM_GA:SM_GA + GLA_RANK].set(w2)
    return out.astype(BF16)


def _rope_tables(pos):
    half = RET_DH // 2
    inv = ROPE_BASE ** (-jnp.arange(half, dtype=F32) / half)
    ang = pos.astype(F32)[:, None] * inv[None, :]
    return jnp.cos(ang), jnp.sin(ang)


def _retention_log_decay_table():
    lg = jnp.log1p(-jnp.exp2(-5.0 - jnp.arange(RET_HEADS, dtype=F32)))
    return jnp.broadcast_to(lg[:, None, None], (RET_HEADS, 1, LANES))


def _dense_tail(x, branches, z, wb, layer):
    merged = _merge(branches, z, wb["w_branch"], layer)
    x = _matmul(merged, wb["w_out"], layer=layer, res=x, name="out_proj")
    u = _matmul(x, wb["w_ff1"], layer=layer, g=wb["g_mlp"][layer], act="relu2", out_dtype=BF16, name="ff1")
    return _matmul(u, wb["w_ff2"], layer=layer, res=x, name="ff2")


def kernel(x_prompt, x_sample, cache_fox_k, cache_fox_v, cache_fox_logf, state_gla, state_ret, cache_mem_k, cache_mem_v, page_table, mem_prompt, g_mix, w_in, b_fox_f, w_gla_a2, b_gla_a, g_gla, g_ret, w_mem_kv, w_branch, w_out, g_mlp, w_ff1, w_ff2, g_final):
    depth = w_in.shape[0]
    bp, t, d = x_prompt.shape
    bs, ts, _ = x_sample.shape
    assert ts == 1 and d == D_MODEL
    n_pool = cache_fox_k.shape[1]
    n_mem = mem_prompt.shape[1]
    past = page_table.shape[1] * PAGE_SIZE
    rows_s = -(-bs // BF16_ROWS) * BF16_ROWS

    cos_p, sin_p = _rope_tables(jnp.arange(t))
    cos_s, sin_s = _rope_tables(past + jnp.arange(ts))
    lg_tab = _retention_log_decay_table()
    k_pool = cache_fox_k.reshape(depth, n_pool, PAGE_SIZE * FOX_HEADS, FOX_DH)
    v_pool = cache_fox_v.reshape(depth, n_pool, PAGE_SIZE * FOX_HEADS, FOX_DH)
    lf_pool = cache_fox_logf.reshape(depth * n_pool, PAGE_SIZE * FOX_HEADS)
    sf_pool = _page_suffix(lf_pool)
    mem_k_s = cache_mem_k.reshape(depth, bs, n_mem, MEM_W)
    mem_v_s = cache_mem_v.reshape(depth, bs, n_mem, MEM_W)
    mem_prompt2 = mem_prompt.reshape(bp * n_mem, d)

    xp = x_prompt.reshape(bp * t, d)
    xs = jnp.zeros((rows_s, d), F32).at[:bs].set(x_sample.reshape(bs, d))
    fk_stack = jnp.zeros((depth, bp * t, FOX_HEADS, FOX_DH), F32)
    fv_stack = jnp.zeros((depth, bp * t, FOX_HEADS, FOX_DH), F32)
    outs = {k: [] for k in ("fl_p", "sg_p", "sr_p", "mk_p", "mv_p", "fk_s", "fv_s", "fl_s", "sg_s", "sr_s")}
    to_bf16_rows = lambda a: jnp.zeros((rows_s, a.size // bs), BF16).at[:bs].set(a.reshape(bs, -1).astype(BF16))

    wb = dict(w_branch=w_branch.astype(BF16), w_out=w_out.astype(BF16), g_mlp=g_mlp,
              w_ff1=w_ff1.astype(BF16), w_ff2=w_ff2.astype(BF16))
    w_mem_kv_b = w_mem_kv.astype(BF16)
    w_in_t = jnp.swapaxes(w_in, 1, 2)
    for l in range(depth):
        w_in_p = _pack_w_in(w_in_t, l)
        w2p = _pack_w_gla_a2(w_gla_a2[l])

        z, zb = _matmul(xp, w_in_p, w_rows=True, g=g_mix[l], bf16_copy=True, name="in_proj")
        z3 = z.reshape(bp, t, N_PACK)
        zb3 = zb.reshape(bp, t, N_PACK)
        ff_t = jnp.swapaxes(z3[:, :, OFF_SM + SM_FF:OFF_SM + SM_FF + FOX_HEADS], 1, 2)
        lf_t, c_t = _fox_gate(ff_t, b_fox_f[l])
        o_fox = _fox_attn(z3, zb3, c_t)
        mkv = _matmul(mem_prompt2, w_mem_kv_b, layer=l, name="mem_kv")
        mkv4 = mkv.reshape(1, bp, n_mem, 2 * MEM_W)
        o_mem = _mem_attn(z3, mkv4, mkv4, layer=0, kcol=0, vcol=1, out_dtype=BF16)
        o_gla, s_gla = _gla_prompt(z3, w2p, b_gla_a[l], g_gla[l])
        o_ret, s_ret = _ret_prompt(z3, cos_p, sin_p, lg_tab, g_ret[l])
        flat = lambda a: a.reshape(bp * t, -1)
        xp = _dense_tail(xp, [flat(o_fox), flat(o_gla), flat(o_ret), flat(o_mem)], zb, wb, l)
        fk_stack, fv_stack = _heads_out(z, fk_stack, fv_stack, l)
        outs["fl_p"].append(jnp.swapaxes(lf_t, 1, 2))
        outs["sg_p"].append(s_gla)
        outs["sr_p"].append(s_ret)
        outs["mk_p"].append(mkv4[0, :, :, :MEM_W].reshape(bp, n_mem, MEM_HEADS, MEM_DH))
        outs["mv_p"].append(mkv4[0, :, :, MEM_W:].reshape(bp, n_mem, MEM_HEADS, MEM_DH))

        zs = _matmul(xs, w_in_p, w_rows=True, g=g_mix[l], name="in_proj_s")
        zs3 = zs[:bs].reshape(bs, 1, N_PACK)
        ff_col = zs[:bs, OFF_SM + SM_FF:OFF_SM + SM_FF + FOX_HEADS].reshape(bs, FOX_HEADS, 1)
        heads = lambda off: zs[:bs, off:off + FOX_W].reshape(bs, FOX_HEADS, FOX_DH)
        o_fox_s, lf_s = _fox_decode(page_table, heads(OFF_FQ), heads(OFF_FK), heads(OFF_FV), ff_col,
                                    b_fox_f[l].reshape(FOX_HEADS, 1), k_pool, v_pool, lf_pool, sf_pool, l)
        o_mem_s = _mem_attn(zs3, mem_k_s, mem_v_s, layer=l, kcol=0, vcol=0, out_dtype=F32)
        o_gla_s, sg_s = _gla_step(zs3, w2p, b_gla_a[l], g_gla[l], state_gla, l)
        o_ret_s, sr_s = _ret_step(zs3, cos_s, sin_s, lg_tab, g_ret[l], state_ret, l)
        xs = _dense_tail(xs, [to_bf16_rows(a) for a in (o_fox_s, o_gla_s, o_ret_s, o_mem_s)], zs, wb, l)
        outs["fk_s"].append(zs3[:, :, OFF_FK:OFF_FK + FOX_W].reshape(bs, 1, FOX_HEADS, FOX_DH))
        outs["fv_s"].append(zs3[:, :, OFF_FV:OFF_FV + FOX_W].reshape(bs, 1, FOX_HEADS, FOX_DH))
        outs["fl_s"].append(lf_s.reshape(bs, 1, FOX_HEADS))
        outs["sg_s"].append(sg_s)
        outs["sr_s"].append(sr_s)

    y_prompt = _rmsnorm(xp, g_final).reshape(bp, t, d)
    y_sample = _rmsnorm(xs, g_final)[:bs].reshape(bs, 1, d)
    st = lambda k: jnp.stack(outs[k])
    fk_p = fk_stack.reshape(depth, bp, t, FOX_HEADS, FOX_DH)
    fv_p = fv_stack.reshape(depth, bp, t, FOX_HEADS, FOX_DH)
    return (y_prompt, y_sample, fk_p, fv_p, st("fl_p"), st("sg_p"), st("sr_p"), st("mk_p"), st("mv_p"),
            st("fk_s"), st("fv_s"), st("fl_s"), st("sg_s"), st("sr_s"))
```
